```python
import jax, jax.numpy as jnp
from jax import lax
import numpy as np

D_MODEL = 1024
BATCH = 4
SEQ = 8192
DEPTH = 1

N_META = 16
D_MIX = D_MODEL
D_CONV = D_MIX // 2
D_POOL = D_MIX - D_CONV
CONV_HEADS = 8
CONV_WIDTH = 3
POOL_WINDOWS = (2, 4, 8, 16)
N_POOL_GROUPS = len(POOL_WINDOWS)
POOL_GROUP_DIM = D_POOL // N_POOL_GROUPS
D_IN_PROJ = 3 * D_CONV + D_POOL
N_EXPERT_GROUPS = 4
EXPERTS_PER_GROUP = 8
N_EXPERTS = N_EXPERT_GROUPS * EXPERTS_PER_GROUP
TOP_K = 2
D_EXPERT = D_MODEL // 4
EPS = 1e-6

kernel_name = 'hymba_conv_pool_hier_moe_layer'


def rmsnorm(x, g):
    xf = x.astype(jnp.float32)
    y = xf * lax.rsqrt(jnp.mean(xf * xf, axis=-1, keepdims=True) + EPS)
    return (y * g.astype(jnp.float32)).astype(x.dtype)


def short_conv_mixer(b_gate, c_gate, v, conv_w):
    L = v.shape[1]
    u = c_gate * v
    up = jnp.pad(u, ((0, 0), (CONV_WIDTH - 1, 0), (0, 0)))
    conv = conv_w[0] * up[:, 0:L]
    for k in range(1, CONV_WIDTH):
        conv = conv + conv_w[k] * up[:, k:k + L]
    return b_gate * conv


def pool_mixer(v, pool_w, pool_scale):
    Bn, L, _ = v.shape
    vf = v.astype(jnp.float32)
    cs = jnp.concatenate([jnp.zeros((Bn, 1, D_POOL), jnp.float32),
                          jnp.cumsum(vf, axis=1)], axis=1)
    pos = jnp.arange(L, dtype=jnp.int32)
    outs = []
    for g, w in enumerate(POOL_WINDOWS):
        sl = slice(g * POOL_GROUP_DIM, (g + 1) * POOL_GROUP_DIM)
        csg = cs[:, :, sl]
        upper = csg[:, 1:]
        lower = jnp.pad(csg[:, :L + 1 - w], ((0, 0), (w - 1, 0), (0, 0)))
        count = jnp.minimum(pos + 1, w).astype(jnp.float32)[None, :, None]
        outs.append((upper - lower) / count - vf[:, :, sl])
    pooled = jnp.stack(outs, axis=2).astype(v.dtype)
    mixed = jnp.einsum('blgc,gcd->blgd', pooled, pool_w)
    return mixed.reshape(Bn, L, D_POOL) * pool_scale


def hierarchical_moe(h, w_router_group, w_router_expert, w_gate, w_up, w_down):
    T = h.shape[0]
    lg = jnp.einsum('td,dg->tg', h, w_router_group).astype(jnp.float32)
    pg = jax.nn.softmax(lg, axis=-1)
    gsel = jnp.argmax(lg, axis=-1)
    g1 = jnp.take_along_axis(pg, gsel[:, None], axis=-1)
    le = jnp.einsum('td,de->te', h, w_router_expert).astype(jnp.float32)
    le = le.reshape(T, N_EXPERT_GROUPS, EXPERTS_PER_GROUP)
    le_sel = jnp.take_along_axis(le, gsel[:, None, None], axis=1)[:, 0]
    pe = jax.nn.softmax(le_sel, axis=-1)
    topv, topi = lax.top_k(pe, TOP_K)
    topv = topv / jnp.sum(topv, axis=-1, keepdims=True)
    within = jnp.sum(jax.nn.one_hot(topi, EXPERTS_PER_GROUP, dtype=jnp.float32)
                     * topv[..., None], axis=1)
    group_gate = jax.nn.one_hot(gsel, N_EXPERT_GROUPS, dtype=jnp.float32) * g1
    gates = (group_gate[:, :, None] * within[:, None, :]).reshape(T, N_EXPERTS).astype(h.dtype)
    out = jnp.zeros_like(h)
    for e in range(N_EXPERTS):
        a = jax.nn.silu(h @ w_gate[e]) * (h @ w_up[e])
        out = out + gates[:, e:e + 1] * (a @ w_down[e])
    return out


def setup_inputs(seed: int = 0) -> dict:
    key = jax.random.key(seed)
    ks = jax.random.split(key, 20)
    f32 = jnp.float32
    nrm = lambda k, shape, s: jax.random.normal(k, shape, f32) * s
    gain = lambda k, shape: 1.0 + 0.02 * jax.random.normal(k, shape, f32)
    return {
        'x': jax.random.normal(ks[0], (BATCH, SEQ, D_MODEL), f32),
        'meta_tokens': nrm(ks[1], (N_META, D_MODEL), 1.0),
        'norm_mix': gain(ks[2], (DEPTH, D_MODEL)),
        'w_in': nrm(ks[3], (DEPTH, D_MODEL, D_IN_PROJ), D_MODEL ** -0.5),
        'conv_w': nrm(ks[4], (DEPTH, CONV_WIDTH, D_CONV), CONV_WIDTH ** -0.5),
        'norm_conv_out': gain(ks[5], (DEPTH, D_CONV)),
        'pool_w': nrm(ks[6], (DEPTH, N_POOL_GROUPS, POOL_GROUP_DIM, POOL_GROUP_DIM), POOL_GROUP_DIM ** -0.5),
        'pool_scale': gain(ks[7], (DEPTH, D_POOL)) + 0.1 * jax.random.normal(ks[8], (DEPTH, D_POOL), f32),
        'norm_pool_out': gain(ks[9], (DEPTH, D_POOL)),
        'w_out': nrm(ks[10], (DEPTH, D_MIX, D_MODEL), D_MIX ** -0.5),
        'norm_ffn': gain(ks[11], (DEPTH, D_MODEL)),
        'w_router_group': nrm(ks[12], (DEPTH, D_MODEL, N_EXPERT_GROUPS), D_MODEL ** -0.5),
        'w_router_expert': nrm(ks[13], (DEPTH, D_MODEL, N_EXPERTS), D_MODEL ** -0.5),
        'w_gate': nrm(ks[14], (DEPTH, N_EXPERTS, D_MODEL, D_EXPERT), D_MODEL ** -0.5),
        'w_up': nrm(ks[15], (DEPTH, N_EXPERTS, D_MODEL, D_EXPERT), D_MODEL ** -0.5),
        'w_down': nrm(ks[16], (DEPTH, N_EXPERTS, D_EXPERT, D_MODEL), D_EXPERT ** -0.5),
        'final_norm': gain(ks[17], (D_MODEL,)),
    }


def reference(x, meta_tokens, norm_mix, w_in, conv_w, norm_conv_out, pool_w, pool_scale,
              norm_pool_out, w_out, norm_ffn, w_router_group, w_router_expert,
              w_gate, w_up, w_down, final_norm):
    Bn = x.shape[0]
    meta = jnp.broadcast_to(meta_tokens.astype(x.dtype)[None], (Bn, N_META, D_MODEL))
    h = jnp.concatenate([meta, x], axis=1)
    L = h.shape[1]
    for layer in range(DEPTH):
        hn = rmsnorm(h, norm_mix[layer])
        proj = jnp.einsum('bld,dc->blc', hn, w_in[layer])
        b_gate = proj[..., 0:D_CONV]
        c_gate = proj[..., D_CONV:2 * D_CONV]
        v_conv = proj[..., 2 * D_CONV:3 * D_CONV]
        v_pool = proj[..., 3 * D_CONV:]
        y_conv = rmsnorm(short_conv_mixer(b_gate, c_gate, v_conv, conv_w[layer]), norm_conv_out[layer])
        y_pool = rmsnorm(pool_mixer(v_pool, pool_w[layer], pool_scale[layer]), norm_pool_out[layer])
        y = jnp.concatenate([y_conv, y_pool], axis=-1)
        h = h + jnp.einsum('blc,cd->bld', y, w_out[layer])
        hf = rmsnorm(h, norm_ffn[layer]).reshape(Bn * L, D_MODEL)
        moe = hierarchical_moe(hf, w_router_group[layer], w_router_expert[layer],
                               w_gate[layer], w_up[layer], w_down[layer])
        h = h + moe.reshape(Bn, L, D_MODEL)
    out = rmsnorm(h, final_norm)
    return out[:, N_META:]
```

```python
import functools

import jax
import jax.numpy as jnp
from jax import lax
from jax.experimental import pallas as pl
from jax.experimental.pallas import tpu as pltpu

EPS = 1e-6
N_META = 16
CONV_WIDTH = 3
POOL_WINDOWS = (2, 4, 8, 16)
N_EXPERT_GROUPS = 4
EXPERTS_PER_GROUP = 8
N_EXPERTS = N_EXPERT_GROUPS * EXPERTS_PER_GROUP
LANES = 128
HALO = 16

MIX_TILE = 512
MOE_TILE = 1024
VMEM_LIMIT = 56 * 1024 * 1024


def _rms(x, g):
    return x * lax.rsqrt(jnp.mean(x * x, axis=-1, keepdims=True) + EPS) * g


def _router_gates(logits):
    lane = lax.broadcasted_iota(jnp.int32, logits.shape, 1)
    neg = jnp.float32(-jnp.inf)
    big = jnp.int32(1 << 20)
    gmask = (lane >= N_EXPERTS) & (lane < N_EXPERTS + N_EXPERT_GROUPS)
    lg = jnp.where(gmask, logits, neg)
    gmax = jnp.max(lg, axis=-1, keepdims=True)
    gden = jnp.sum(jnp.exp(lg - gmax), axis=-1, keepdims=True)
    g1 = 1.0 / gden
    gsel = jnp.min(jnp.where(lg == gmax, lane, big), axis=-1, keepdims=True) - N_EXPERTS
    emask = (lane < N_EXPERTS) & ((lane // EXPERTS_PER_GROUP) == gsel)
    le = jnp.where(emask, logits, neg)
    m1 = jnp.max(le, axis=-1, keepdims=True)
    pe = jnp.exp(le - m1)
    i1 = jnp.min(jnp.where(le == m1, lane, big), axis=-1, keepdims=True)
    pe2 = jnp.where(lane == i1, -1.0, pe)
    p2 = jnp.max(pe2, axis=-1, keepdims=True)
    i2 = jnp.min(jnp.where((pe2 == p2) & emask, lane, big), axis=-1, keepdims=True)
    inv = 1.0 / (1.0 + p2)
    w = jnp.where(lane == i1, inv, 0.0) + jnp.where(lane == i2, p2 * inv, 0.0)
    return w * g1


def _mixer_kernel(x_ref, meta_ref, nmix_ref, win_ref, convw_ref, nconv_ref, poolw_ref,
                  pscale_ref, npool_ref, wout_ref, nffn_ref, wr_ref,
                  h1_ref, hf_ref, gates_ref, extu_ref, extp_ref, *, tm, d_conv):
    i = pl.program_id(1)
    nmix = nmix_ref[...]

    @pl.when(i == 0)
    def _():
        hm = _rms(meta_ref[...], nmix).astype(jnp.bfloat16)
        pm = jnp.dot(hm, win_ref[:, d_conv:], preferred_element_type=jnp.float32)
        extu_ref[0:HALO, :] = pm[:, 0:d_conv] * pm[:, d_conv:2 * d_conv]
        extp_ref[0:HALO, :] = pm[:, 2 * d_conv:]

    x = x_ref[0]
    hn = _rms(x, nmix).astype(jnp.bfloat16)
    proj = jnp.dot(hn, win_ref[...], preferred_element_type=jnp.float32)
    b_gate = proj[:, 0:d_conv]
    v_pool = proj[:, 3 * d_conv:]
    extu_ref[HALO:, :] = proj[:, d_conv:2 * d_conv] * proj[:, 2 * d_conv:3 * d_conv]
    extp_ref[HALO:, :] = v_pool

    conv = convw_ref[CONV_WIDTH - 1:CONV_WIDTH, :] * extu_ref[HALO:, :]
    for k in range(CONV_WIDTH - 1):
        off = HALO - (CONV_WIDTH - 1 - k)
        conv = conv + convw_ref[k:k + 1, :] * extu_ref[off:off + tm, :]
    y_conv = _rms(b_gate * conv, nconv_ref[...])

    cg = LANES
    mixed = []
    for g, w in enumerate(POOL_WINDOWS):
        cols = slice(g * cg, (g + 1) * cg)
        s = extp_ref[HALO:, cols]
        for j in range(1, w):
            s = s + extp_ref[HALO - j:HALO - j + tm, cols]
        pooled = s * (1.0 / w) - v_pool[:, cols]
        mixed.append(jnp.dot(pooled.astype(jnp.bfloat16), poolw_ref[g],
                             preferred_element_type=jnp.float32))
    y_pool = _rms(jnp.concatenate(mixed, axis=-1) * pscale_ref[...], npool_ref[...])

    y = jnp.concatenate([y_conv, y_pool], axis=-1).astype(jnp.bfloat16)
    h1 = x + jnp.dot(y, wout_ref[...], preferred_element_type=jnp.float32)
    h1_ref[0] = h1
    hf = _rms(h1, nffn_ref[...]).astype(jnp.bfloat16)
    hf_ref[0] = hf
    logits = jnp.dot(hf, wr_ref[...], preferred_element_type=jnp.float32)
    gates_ref[0] = _router_gates(logits)

    extu_ref[0:HALO, :] = extu_ref[tm:tm + HALO, :]
    extp_ref[0:HALO, :] = extp_ref[tm:tm + HALO, :]


def _moe_kernel(hf_ref, gates_ref, h1_ref, wgu_ref, wd_ref, fn_ref, out_ref, acc_ref, *, d_exp):
    e = pl.program_id(1)

    @pl.when(e == 0)
    def _():
        acc_ref[...] = jnp.zeros_like(acc_ref)

    gu = jnp.dot(hf_ref[...], wgu_ref[0], preferred_element_type=jnp.float32)
    g = gu[:, :d_exp]
    a = g * jax.nn.sigmoid(g) * gu[:, d_exp:]
    gates = gates_ref[...]
    lane = lax.broadcasted_iota(jnp.int32, gates.shape, 1)
    gcol = jnp.sum(jnp.where(lane == e, gates, 0.0), axis=-1, keepdims=True)
    acc_ref[...] += jnp.dot((a * gcol).astype(jnp.bfloat16), wd_ref[0],
                            preferred_element_type=jnp.float32)

    @pl.when(e == pl.num_programs(1) - 1)
    def _():
        out_ref[...] = _rms(h1_ref[...] + acc_ref[...], fn_ref[...])


def kernel(x, meta_tokens, norm_mix, w_in, conv_w, norm_conv_out, pool_w, pool_scale,
           norm_pool_out, w_out, norm_ffn, w_router_group, w_router_expert,
           w_gate, w_up, w_down, final_norm):
    bsz, seq, d = x.shape
    d_conv = conv_w.shape[-1]
    d_exp = w_gate.shape[-1]
    n_exp = w_gate.shape[1]
    assert meta_tokens.shape[0] == N_META == HALO and n_exp == N_EXPERTS
    assert seq % MIX_TILE == 0 and (bsz * seq) % MOE_TILE == 0
    bf = jnp.bfloat16
    row = lambda a: a.reshape(1, -1)
    wr = jnp.concatenate(
        [w_router_expert[0], w_router_group[0],
         jnp.zeros((d, LANES - N_EXPERTS - N_EXPERT_GROUPS), jnp.float32)], axis=1).astype(bf)

    full = lambda shape: pl.BlockSpec(shape, lambda b, i: (0,) * len(shape))
    tile = lambda width: pl.BlockSpec((1, MIX_TILE, width), lambda b, i: (b, i, 0))
    h1, hf, gates = pl.pallas_call(
        functools.partial(_mixer_kernel, tm=MIX_TILE, d_conv=d_conv),
        grid=(bsz, seq // MIX_TILE),
        in_specs=[tile(d), full((N_META, d)), full((1, d)), full((d, 4 * d_conv)),
                  full((CONV_WIDTH, d_conv)), full((1, d_conv)),
                  full((len(POOL_WINDOWS), LANES, LANES)), full((1, d_conv)), full((1, d_conv)),
                  full((2 * d_conv, d)), full((1, d)), full((d, LANES))],
        out_specs=[tile(d), tile(d), tile(LANES)],
        out_shape=[jax.ShapeDtypeStruct((bsz, seq, d), jnp.float32),
                   jax.ShapeDtypeStruct((bsz, seq, d), bf),
                   jax.ShapeDtypeStruct((bsz, seq, LANES), jnp.float32)],
        scratch_shapes=[pltpu.VMEM((MIX_TILE + HALO, d_conv), jnp.float32),
                        pltpu.VMEM((MIX_TILE + HALO, d_conv), jnp.float32)],
        compiler_params=pltpu.CompilerParams(
            dimension_semantics=("arbitrary", "arbitrary"), vmem_limit_bytes=VMEM_LIMIT),
        name="mixer",
    )(x, meta_tokens, row(norm_mix), w_in[0].astype(bf), conv_w[0], row(norm_conv_out),
      pool_w[0].astype(bf), row(pool_scale), row(norm_pool_out), w_out[0].astype(bf),
      row(norm_ffn), wr)

    t = bsz * seq
    wgu = jnp.concatenate([w_gate[0], w_up[0]], axis=-1).astype(bf)
    out = pl.pallas_call(
        functools.partial(_moe_kernel, d_exp=d_exp),
        grid=(t // MOE_TILE, n_exp),
        in_specs=[pl.BlockSpec((MOE_TILE, d), lambda i, e: (i, 0)),
                  pl.BlockSpec((MOE_TILE, LANES), lambda i, e: (i, 0)),
                  pl.BlockSpec((MOE_TILE, d), lambda i, e: (i, 0)),
                  pl.BlockSpec((1, d, 2 * d_exp), lambda i, e: (e, 0, 0)),
                  pl.BlockSpec((1, d_exp, d), lambda i, e: (e, 0, 0)),
                  pl.BlockSpec((1, d), lambda i, e: (0, 0))],
        out_specs=pl.BlockSpec((MOE_TILE, d), lambda i, e: (i, 0)),
        out_shape=jax.ShapeDtypeStruct((t, d), jnp.float32),
        scratch_shapes=[pltpu.VMEM((MOE_TILE, d), jnp.float32)],
        compiler_params=pltpu.CompilerParams(
            dimension_semantics=("arbitrary", "arbitrary"), vmem_limit_bytes=VMEM_LIMIT),
        name="moe",
    )(hf.reshape(t, d), gates.reshape(t, LANES), h1.reshape(t, d), wgu,
      w_down[0].astype(bf), row(final_norm))
    return out.reshape(bsz, seq, d)
```

```python
import functools

import jax
import jax.numpy as jnp
from jax import lax
from jax.experimental import pallas as pl
from jax.experimental.pallas import tpu as pltpu

EPS = 1e-6
N_META = 16
CONV_WIDTH = 3
POOL_WINDOWS = (2, 4, 8, 16)
N_EXPERT_GROUPS = 4
EXPERTS_PER_GROUP = 8
N_EXPERTS = N_EXPERT_GROUPS * EXPERTS_PER_GROUP
LANES = 128
SUBLANES = 8
HALO = 16
CHUNK = 16
ROUTER_ROWS = 64

MIX_TILE = 512
PERM_BLK = 256
MOE_TILE = 256
VMEM_LIMIT = 56 * 1024 * 1024

STAGE_ROWS = -(-(2 * MIX_TILE + N_EXPERTS * CHUNK) // PERM_BLK) * PERM_BLK
TAB_STRIDE = 2 * N_EXPERTS


def _rms(x, g):
    return x * lax.rsqrt(jnp.mean(x * x, axis=-1, keepdims=True) + EPS) * g


def _route(lt):
    tm = lt.shape[1]
    sub = lax.broadcasted_iota(jnp.int32, (SUBLANES, tm), 0)
    neg = jnp.float32(-jnp.inf)
    big = jnp.int32(1 << 20)
    lg = jnp.where(sub < N_EXPERT_GROUPS, lt[N_EXPERTS:N_EXPERTS + SUBLANES], neg)
    gmax = jnp.max(lg, axis=0, keepdims=True)
    g1 = 1.0 / jnp.sum(jnp.exp(lg - gmax), axis=0, keepdims=True)
    gsel = jnp.min(jnp.where(lg == gmax, sub, big), axis=0, keepdims=True)
    le = lt[(N_EXPERT_GROUPS - 1) * SUBLANES:N_EXPERT_GROUPS * SUBLANES]
    for g in range(N_EXPERT_GROUPS - 2, -1, -1):
        le = jnp.where(gsel == g, lt[g * SUBLANES:(g + 1) * SUBLANES], le)
    m1 = jnp.max(le, axis=0, keepdims=True)
    pe = jnp.exp(le - m1)
    i1 = jnp.min(jnp.where(le == m1, sub, big), axis=0, keepdims=True)
    pe2 = jnp.where(sub == i1, -1.0, pe)
    p2 = jnp.max(pe2, axis=0, keepdims=True)
    i2 = jnp.min(jnp.where(pe2 == p2, sub, big), axis=0, keepdims=True)
    inv = 1.0 / (1.0 + p2)
    return (gsel * EXPERTS_PER_GROUP + i1, gsel * EXPERTS_PER_GROUP + i2,
            g1 * inv, g1 * p2 * inv)


def _chunk_copy(src_ref, src_row, dst_ref, dst_row, sem):
    aligned = lambda r: r if isinstance(r, int) else pl.multiple_of(r, CHUNK)
    return pltpu.make_async_copy(
        src_ref.at[pl.ds(aligned(src_row), CHUNK), :],
        dst_ref.at[pl.ds(aligned(dst_row), CHUNK), :], sem)


def _mixer_kernel(x_ref, meta_ref, nmix_ref, win_ref, convw_ref, nconv_ref, poolw_ref,
                  pscale_ref, npool_ref, wout_ref, nffn_ref, wrt_ref, tri_ref,
                  h1_ref, route_ref, xs_ref, tab_ref, cnt_ref,
                  extu_ref, extp_ref, hf_ref, stage_ref, zero_ref, info_v_ref, info_s_ref,
                  run_ref, pend_ref, sem_info, sem_out, *, tm, d_conv, cap, moe_tile):
    i = pl.program_id(1)
    step = pl.program_id(0) * pl.num_programs(1) + i
    n_steps = pl.num_programs(0) * pl.num_programs(1)
    nmix = nmix_ref[...]

    @pl.when(step == 0)
    def _():
        pend_ref[0] = 0
        zero_ref[...] = jnp.zeros_like(zero_ref)

        def init(e, c):
            run_ref[e] = 0
            return c
        lax.fori_loop(0, N_EXPERTS, init, 0)

    @pl.when(i == 0)
    def _():
        hm = _rms(meta_ref[...], nmix).astype(jnp.bfloat16)
        pm = jnp.dot(hm, win_ref[:, d_conv:], preferred_element_type=jnp.float32)
        extu_ref[0:HALO, :] = pm[:, 0:d_conv] * pm[:, d_conv:2 * d_conv]
        extp_ref[0:HALO, :] = pm[:, 2 * d_conv:]

    x = x_ref[0]
    hn = _rms(x, nmix).astype(jnp.bfloat16)
    proj = jnp.dot(hn, win_ref[...], preferred_element_type=jnp.float32)
    b_gate = proj[:, 0:d_conv]
    v_pool = proj[:, 3 * d_conv:]
    extu_ref[HALO:, :] = proj[:, d_conv:2 * d_conv] * proj[:, 2 * d_conv:3 * d_conv]
    extp_ref[HALO:, :] = v_pool

    conv = convw_ref[CONV_WIDTH - 1:CONV_WIDTH, :] * extu_ref[HALO:, :]
    for k in range(CONV_WIDTH - 1):
        off = HALO - (CONV_WIDTH - 1 - k)
        conv = conv + convw_ref[k:k + 1, :] * extu_ref[off:off + tm, :]
    y_conv = _rms(b_gate * conv, nconv_ref[...])

    mixed = []
    for g, w in enumerate(POOL_WINDOWS):
        cols = slice(g * LANES, (g + 1) * LANES)
        s = extp_ref[HALO:, cols]
        for j in range(1, w):
            s = s + extp_ref[HALO - j:HALO - j + tm, cols]
        pooled = s * (1.0 / w) - v_pool[:, cols]
        mixed.append(jnp.dot(pooled.astype(jnp.bfloat16), poolw_ref[g],
                             preferred_element_type=jnp.float32))
    y_pool = _rms(jnp.concatenate(mixed, axis=-1) * pscale_ref[...], npool_ref[...])

    y = jnp.concatenate([y_conv, y_pool], axis=-1).astype(jnp.bfloat16)
    h1 = x + jnp.dot(y, wout_ref[...], preferred_element_type=jnp.float32)
    h1_ref[0] = h1
    hf = _rms(h1, nffn_ref[...]).astype(jnp.bfloat16)
    hf_ref[...] = hf

    extu_ref[0:HALO, :] = extu_ref[tm:tm + HALO, :]
    extp_ref[0:HALO, :] = extp_ref[tm:tm + HALO, :]

    lt = lax.dot_general(wrt_ref[...], hf, (((1,), (1,)), ((), ())),
                         preferred_element_type=jnp.float32)
    e0, e1, gate0, gate1 = _route(lt)
    esub = lax.broadcasted_iota(jnp.int32, (N_EXPERTS, tm), 0)
    oh0 = esub == e0
    oh1 = esub == e1
    oh = jnp.where(oh0, 1.0, jnp.where(oh1, 1.0, 0.0))
    ct = jnp.dot(oh.astype(jnp.bfloat16), tri_ref[...], preferred_element_type=jnp.float32)
    n_col = jnp.sum(oh, axis=1, keepdims=True)
    npad_col = jnp.floor((n_col + (CHUNK - 1)) * (1.0 / CHUNK)) * CHUNK
    e_sub = lax.broadcasted_iota(jnp.int32, (N_EXPERTS, LANES), 0)
    l_lane = lax.broadcasted_iota(jnp.int32, (N_EXPERTS, LANES), 1)
    npad_row = jnp.sum(jnp.where(e_sub == l_lane, npad_col, 0.0), axis=0, keepdims=True)
    loff_col = jnp.sum(jnp.where(l_lane < e_sub, npad_row, 0.0), axis=1, keepdims=True)
    loff_row = jnp.sum(jnp.where(e_sub == l_lane, loff_col, 0.0), axis=0, keepdims=True)
    total = jnp.sum(npad_col, axis=0, keepdims=True)
    info_v_ref[...] = jnp.concatenate(
        [npad_row, loff_row, jnp.zeros((SUBLANES - 2, LANES), jnp.float32)], axis=0
    ).astype(jnp.int32)
    info_cp = pltpu.make_async_copy(info_v_ref, info_s_ref, sem_info)
    info_cp.start()

    posbase = ct + loff_col
    pos0 = jnp.sum(jnp.where(oh0, posbase, 0.0), axis=0, keepdims=True)
    pos1 = jnp.sum(jnp.where(oh1, posbase, 0.0), axis=0, keepdims=True)
    route_t = jnp.concatenate(
        [pos0, pos1, gate0, gate1, jnp.zeros((LANES - 4, tm), jnp.float32)], axis=0)
    route_ref[0] = route_t.T

    slot = step % 2
    pos0i = pos0.astype(jnp.int32)
    pos1i = pos1.astype(jnp.int32)
    n_rows = total[0, 0].astype(jnp.int32)
    n_blk = (n_rows + (PERM_BLK - 1)) // PERM_BLK

    def perm_body(j, c):
        r = lax.broadcasted_iota(jnp.int32, (PERM_BLK, tm), 0) + j * PERM_BLK
        p = jnp.where(r == pos0i, 1.0, jnp.where(r == pos1i, 1.0, 0.0)).astype(jnp.bfloat16)
        rows = jnp.dot(p, hf_ref[...], preferred_element_type=jnp.float32)
        stage_ref[slot, pl.ds(pl.multiple_of(j * PERM_BLK, PERM_BLK), PERM_BLK), :] = (
            rows.astype(jnp.bfloat16))
        return c
    lax.fori_loop(0, n_blk, perm_body, 0)

    def wait_chunks(n):
        def body(k, c):
            _chunk_copy(zero_ref, 0, xs_ref, 0, sem_out).wait()
            return c
        lax.fori_loop(0, n, body, 0)

    info_cp.wait()
    wait_chunks(pend_ref[0])

    def expert_body(e, tot):
        npad = info_s_ref[0, e]
        loff = info_s_ref[1, e]
        base = run_ref[e]
        tab_ref[step * TAB_STRIDE + e] = base
        tab_ref[step * TAB_STRIDE + N_EXPERTS + e] = npad
        nch = npad // CHUNK

        def chunk_body(k, c):
            _chunk_copy(stage_ref.at[slot], loff + k * CHUNK,
                        xs_ref, e * cap + base + k * CHUNK, sem_out).start()
            return c
        lax.fori_loop(0, nch, chunk_body, 0)
        run_ref[e] = base + npad
        return tot + nch
    pend_ref[0] = lax.fori_loop(0, N_EXPERTS, expert_body, 0)

    @pl.when(step == n_steps - 1)
    def _():
        wait_chunks(pend_ref[0])

        def pad_body(e, tot):
            n = run_ref[e]
            target = ((n + (moe_tile - 1)) // moe_tile) * moe_tile
            cnt_ref[e] = target
            nch = (target - n) // CHUNK

            def chunk_body(k, c):
                _chunk_copy(zero_ref, 0, xs_ref, e * cap + n + k * CHUNK, sem_out).start()
                return c
            lax.fori_loop(0, nch, chunk_body, 0)
            return tot + nch
        wait_chunks(lax.fori_loop(0, N_EXPERTS, pad_body, 0))


def _moe_kernel(exp_ref, blk_ref, n_ref, xs_ref, wgu_ref, wd_ref, ys_ref, *, d_exp):
    @pl.when(pl.program_id(0) < n_ref[0])
    def _():
        gu = jnp.dot(xs_ref[...], wgu_ref[0], preferred_element_type=jnp.float32)
        g = gu[:, :d_exp]
        a = g * jax.nn.sigmoid(g) * gu[:, d_exp:]
        ys_ref[...] = jnp.dot(a.astype(jnp.bfloat16), wd_ref[0],
                              preferred_element_type=jnp.float32).astype(jnp.bfloat16)


def _combine_kernel(tab_ref, h1_ref, route_ref, fn_ref, ys_ref, out_ref, stage_ref, sem,
                    *, tm, cap):
    s = pl.program_id(0)
    n_steps = pl.num_programs(0)
    slot = s % 2

    def tile_chunks(j):
        def body(e, tot):
            return tot + tab_ref[j * TAB_STRIDE + N_EXPERTS + e] // CHUNK
        return lax.fori_loop(0, N_EXPERTS, body, 0)

    def issue(j, jslot):
        def body(e, loff):
            base = tab_ref[j * TAB_STRIDE + e]
            npad = tab_ref[j * TAB_STRIDE + N_EXPERTS + e]

            def chunk_body(k, c):
                _chunk_copy(ys_ref, e * cap + base + k * CHUNK,
                            stage_ref.at[jslot], loff + k * CHUNK, sem.at[jslot]).start()
                return c
            lax.fori_loop(0, npad // CHUNK, chunk_body, 0)
            return loff + npad
        lax.fori_loop(0, N_EXPERTS, body, 0)

    @pl.when(s == 0)
    def _():
        stage_ref[...] = jnp.zeros_like(stage_ref)
        issue(0, 0)

    @pl.when(s + 1 < n_steps)
    def _():
        issue(s + 1, (s + 1) % 2)

    n_ch = tile_chunks(s)

    def wait_body(k, c):
        _chunk_copy(ys_ref, 0, stage_ref.at[slot], 0, sem.at[slot]).wait()
        return c
    lax.fori_loop(0, n_ch, wait_body, 0)

    route = route_ref[...]
    pos0 = route[:, 0:1].astype(jnp.int32)
    pos1 = route[:, 1:2].astype(jnp.int32)
    gate0 = route[:, 2:3]
    gate1 = route[:, 3:4]
    out_ref[...] = h1_ref[...]
    n_blk = (n_ch * CHUNK + (PERM_BLK - 1)) // PERM_BLK

    def blk_body(j, c):
        r = lax.broadcasted_iota(jnp.int32, (tm, PERM_BLK), 1) + j * PERM_BLK
        pw = jnp.where(r == pos0, gate0, jnp.where(r == pos1, gate1, 0.0)).astype(jnp.bfloat16)
        rows = stage_ref[slot, pl.ds(pl.multiple_of(j * PERM_BLK, PERM_BLK), PERM_BLK), :]
        out_ref[...] += jnp.dot(pw, rows, preferred_element_type=jnp.float32)
        return c
    lax.fori_loop(0, n_blk, blk_body, 0)
    out_ref[...] = _rms(out_ref[...], fn_ref[...])


def kernel(x, meta_tokens, norm_mix, w_in, conv_w, norm_conv_out, pool_w, pool_scale,
           norm_pool_out, w_out, norm_ffn, w_router_group, w_router_expert,
           w_gate, w_up, w_down, final_norm):
    bsz, seq, d = x.shape
    d_conv = conv_w.shape[-1]
    d_exp = w_gate.shape[-1]
    n_exp = w_gate.shape[1]
    assert meta_tokens.shape[0] == N_META == HALO and n_exp == N_EXPERTS
    assert seq % MIX_TILE == 0
    bf = jnp.bfloat16
    t = bsz * seq
    n_tiles = t // MIX_TILE
    cap = -(-(t + CHUNK * n_tiles) // MOE_TILE) * MOE_TILE + MOE_TILE
    max_work = (2 * t + N_EXPERTS * (CHUNK - 1) * n_tiles) // MOE_TILE + N_EXPERTS
    row = lambda a: a.reshape(1, -1)
    wrt = jnp.concatenate(
        [w_router_expert[0].T, w_router_group[0].T,
         jnp.zeros((ROUTER_ROWS - N_EXPERTS - N_EXPERT_GROUPS, d), jnp.float32)], axis=0).astype(bf)
    tri = (lax.broadcasted_iota(jnp.int32, (MIX_TILE, MIX_TILE), 0)
           < lax.broadcasted_iota(jnp.int32, (MIX_TILE, MIX_TILE), 1)).astype(bf)

    full = lambda shape: pl.BlockSpec(shape, lambda b, i: (0,) * len(shape))
    tile = lambda width: pl.BlockSpec((1, MIX_TILE, width), lambda b, i: (b, i, 0))
    smem = pl.BlockSpec(memory_space=pltpu.SMEM)
    h1, route, xs, tab, cnt = pl.pallas_call(
        functools.partial(_mixer_kernel, tm=MIX_TILE, d_conv=d_conv, cap=cap, moe_tile=MOE_TILE),
        grid=(bsz, seq // MIX_TILE),
        in_specs=[tile(d), full((N_META, d)), full((1, d)), full((d, 4 * d_conv)),
                  full((CONV_WIDTH, d_conv)), full((1, d_conv)),
                  full((len(POOL_WINDOWS), LANES, LANES)), full((1, d_conv)), full((1, d_conv)),
                  full((2 * d_conv, d)), full((1, d)), full((ROUTER_ROWS, d)),
                  full((MIX_TILE, MIX_TILE))],
        out_specs=[tile(d), tile(LANES), pl.BlockSpec(memory_space=pl.ANY), smem, smem],
        out_shape=[jax.ShapeDtypeStruct((bsz, seq, d), jnp.float32),
                   jax.ShapeDtypeStruct((bsz, seq, LANES), jnp.float32),
                   jax.ShapeDtypeStruct((N_EXPERTS * cap, d), bf),
                   jax.ShapeDtypeStruct((n_tiles * TAB_STRIDE,), jnp.int32),
                   jax.ShapeDtypeStruct((N_EXPERTS,), jnp.int32)],
        scratch_shapes=[pltpu.VMEM((MIX_TILE + HALO, d_conv), jnp.float32),
                        pltpu.VMEM((MIX_TILE + HALO, d_conv), jnp.float32),
                        pltpu.VMEM((MIX_TILE, d), bf),
                        pltpu.VMEM((2, STAGE_ROWS, d), bf),
                        pltpu.VMEM((CHUNK, d), bf),
                        pltpu.VMEM((SUBLANES, LANES), jnp.int32),
                        pltpu.SMEM((SUBLANES, LANES), jnp.int32),
                        pltpu.SMEM((N_EXPERTS,), jnp.int32),
                        pltpu.SMEM((1,), jnp.int32),
                        pltpu.SemaphoreType.DMA,
                        pltpu.SemaphoreType.DMA],
        compiler_params=pltpu.CompilerParams(
            dimension_semantics=("arbitrary", "arbitrary"), vmem_limit_bytes=VMEM_LIMIT),
        name="mixer",
    )(x, meta_tokens, row(norm_mix), w_in[0].astype(bf), conv_w[0], row(norm_conv_out),
      pool_w[0].astype(bf), row(pool_scale), row(norm_pool_out), w_out[0].astype(bf),
      row(norm_ffn), wrt, tri)

    tiles_e = cnt // MOE_TILE
    cum = jnp.cumsum(tiles_e)
    n_work = cum[-1:]
    wi = jnp.minimum(jnp.arange(max_work, dtype=jnp.int32), n_work - 1)
    exp_w = jnp.sum((wi[:, None] >= cum[None, :]).astype(jnp.int32), axis=1)
    blk_w = exp_w * (cap // MOE_TILE) + wi - (cum - tiles_e)[exp_w]

    wgu = jnp.concatenate([w_gate[0], w_up[0]], axis=-1).astype(bf)
    ys = pl.pallas_call(
        functools.partial(_moe_kernel, d_exp=d_exp),
        grid_spec=pltpu.PrefetchScalarGridSpec(
            num_scalar_prefetch=3,
            grid=(max_work,),
            in_specs=[pl.BlockSpec((MOE_TILE, d), lambda w, ex, bk, n: (bk[w], 0)),
                      pl.BlockSpec((1, d, 2 * d_exp), lambda w, ex, bk, n: (ex[w], 0, 0)),
                      pl.BlockSpec((1, d_exp, d), lambda w, ex, bk, n: (ex[w], 0, 0))],
            out_specs=pl.BlockSpec((MOE_TILE, d), lambda w, ex, bk, n: (bk[w], 0))),
        out_shape=jax.ShapeDtypeStruct((N_EXPERTS * cap, d), bf),
        compiler_params=pltpu.CompilerParams(
            dimension_semantics=("arbitrary",), vmem_limit_bytes=VMEM_LIMIT),
        name="moe",
    )(exp_w, blk_w, n_work, xs, wgu, w_down[0].astype(bf))

    out = pl.pallas_call(
        functools.partial(_combine_kernel, tm=MIX_TILE, cap=cap),
        grid_spec=pltpu.PrefetchScalarGridSpec(
            num_scalar_prefetch=1,
            grid=(n_tiles,),
            in_specs=[pl.BlockSpec((MIX_TILE, d), lambda s, tb: (s, 0)),
                      pl.BlockSpec((MIX_TILE, LANES), lambda s, tb: (s, 0)),
                      pl.BlockSpec((1, d), lambda s, tb: (0, 0)),
                      pl.BlockSpec(memory_space=pl.ANY)],
            out_specs=pl.BlockSpec((MIX_TILE, d), lambda s, tb: (s, 0)),
            scratch_shapes=[pltpu.VMEM((2, STAGE_ROWS, d), bf),
                            pltpu.SemaphoreType.DMA((2,))]),
        out_shape=jax.ShapeDtypeStruct((t, d), jnp.float32),
        compiler_params=pltpu.CompilerParams(
            dimension_semantics=("arbitrary",), vmem_limit_bytes=VMEM_LIMIT),
        name="combine",
    )(tab, h1.reshape(t, d), route.reshape(t, LANES), row(final_norm), ys)
    return out.reshape(bsz, seq, d)
```

```python
import functools

import jax
import jax.numpy as jnp
from jax import lax
from jax.experimental import pallas as pl
from jax.experimental.pallas import tpu as pltpu

EPS = 1e-6
N_META = 16
CONV_WIDTH = 3
POOL_WINDOWS = (2, 4, 8, 16)
N_EXPERT_GROUPS = 4
EXPERTS_PER_GROUP = 8
N_EXPERTS = N_EXPERT_GROUPS * EXPERTS_PER_GROUP
LANES = 128
SUBLANES = 8
HALO = 16
CHUNK = 16
ROUTER_ROWS = 64

MIX_TILE = 512
MOE_TILE = 256
MOE_CHUNKS = MOE_TILE // CHUNK
MOE_CHUNKS_LOG2 = MOE_CHUNKS.bit_length() - 1
VMEM_LIMIT = 56 * 1024 * 1024

STAGE_ROWS = 2 * MIX_TILE + N_EXPERTS * CHUNK
STAGE_CHUNKS = STAGE_ROWS // CHUNK
PERM_ROWS = 512
TAB_STRIDE = 2 * N_EXPERTS + 1
UNROLLED_CHUNKS = 5
WAIT_GROUP = 16
WAIT_GROUP_LOG2 = WAIT_GROUP.bit_length() - 1


def _rms(x, g):
    return x * lax.rsqrt(jnp.mean(x * x, axis=-1, keepdims=True) + EPS) * g


def _route(lt):
    tm = lt.shape[1]
    sub = lax.broadcasted_iota(jnp.int32, (SUBLANES, tm), 0)
    neg = jnp.float32(-jnp.inf)
    big = jnp.int32(1 << 20)
    lg = jnp.where(sub < N_EXPERT_GROUPS, lt[N_EXPERTS:N_EXPERTS + SUBLANES], neg)
    gmax = jnp.max(lg, axis=0, keepdims=True)
    g1 = 1.0 / jnp.sum(jnp.exp(lg - gmax), axis=0, keepdims=True)
    gsel = jnp.min(jnp.where(lg == gmax, sub, big), axis=0, keepdims=True)
    le = lt[(N_EXPERT_GROUPS - 1) * SUBLANES:N_EXPERT_GROUPS * SUBLANES]
    for g in range(N_EXPERT_GROUPS - 2, -1, -1):
        le = jnp.where(gsel == g, lt[g * SUBLANES:(g + 1) * SUBLANES], le)
    m1 = jnp.max(le, axis=0, keepdims=True)
    pe = jnp.exp(le - m1)
    i1 = jnp.min(jnp.where(le == m1, sub, big), axis=0, keepdims=True)
    pe2 = jnp.where(sub == i1, -1.0, pe)
    p2 = jnp.max(pe2, axis=0, keepdims=True)
    i2 = jnp.min(jnp.where(pe2 == p2, sub, big), axis=0, keepdims=True)
    inv = 1.0 / (1.0 + p2)
    return (gsel * EXPERTS_PER_GROUP + i1, gsel * EXPERTS_PER_GROUP + i2,
            g1 * inv, g1 * p2 * inv)


def _copy_chunks(src_ref, src_chunk, dst_ref, dst_chunk, sem, n=1):
    return pltpu.make_async_copy(src_ref.at[pl.ds(src_chunk, n)],
                                 dst_ref.at[pl.ds(dst_chunk, n)], sem)


def _drain(n_chunks, src_ref, dst_ref, sem):
    def group_body(k, c):
        _copy_chunks(src_ref, 0, dst_ref, 0, sem, WAIT_GROUP).wait()
        return c
    lax.fori_loop(0, lax.shift_right_logical(n_chunks, WAIT_GROUP_LOG2), group_body, 0)

    def one_body(k, c):
        _copy_chunks(src_ref, 0, dst_ref, 0, sem).wait()
        return c
    lax.fori_loop(0, n_chunks & (WAIT_GROUP - 1), one_body, 0)


def _issue_head(nch, src_ref, src0, dst_ref, dst0, sem):
    for k in range(UNROLLED_CHUNKS):
        @pl.when(k < nch)
        def _():
            _copy_chunks(src_ref, src0 + k, dst_ref, dst0 + k, sem).start()


def _issue_rest(first, nch, src_ref, src0, dst_ref, dst0, sem):
    def body(k, c):
        _copy_chunks(src_ref, src0 + k, dst_ref, dst0 + k, sem).start()
        return c
    lax.fori_loop(first, nch, body, 0)


def _mixer_kernel(x_ref, meta_ref, nmix_ref, win_ref, convw_ref, nconv_ref, poolw_ref,
                  pscale_ref, npool_ref, wout_ref, nffn_ref, wrt_ref, tri_ref,
                  h1_ref, route_ref, xs_ref, tab_ref, wexp_ref, wblk_ref, nwork_ref,
                  extu_ref, extp_ref, hf_ref, stage_ref, zero_ref, info_v_ref, info_s_ref,
                  run_ref, pend_ref, sem_info, sem_out,
                  *, tm, d_conv, cap_chunks, tiles_per_seq, max_work):
    s = pl.program_id(0)
    n_tiles = pl.num_programs(0) - 1
    cur = s % 2
    prv = 1 - cur
    st_sort = (s + 2) % 3
    st_disp = (s + 1) % 3
    nmix = nmix_ref[...]

    def info_copy(slot):
        return pltpu.make_async_copy(info_v_ref, info_s_ref.at[slot], sem_info)

    @pl.when(s == 0)
    def _():
        pend_ref[0] = 0
        pend_ref[1] = 0
        zero_ref[...] = jnp.zeros_like(zero_ref)
        hf_ref[...] = jnp.zeros_like(hf_ref)
        info_v_ref[...] = jnp.zeros_like(info_v_ref)
        info_copy(0).start()
        info_copy(0).wait()
        info_copy(1).start()

        def init(e, c):
            run_ref[e] = 0
            return c
        lax.fori_loop(0, N_EXPERTS, init, 0)

    @pl.when(s % tiles_per_seq == 0)
    def _():
        hm = _rms(meta_ref[...], nmix).astype(jnp.bfloat16)
        pm = jnp.dot(hm, win_ref[:, d_conv:], preferred_element_type=jnp.float32)
        extu_ref[0:HALO, :] = pm[:, 0:d_conv] * pm[:, d_conv:2 * d_conv]
        extp_ref[0:HALO, :] = pm[:, 2 * d_conv:]

    info_copy(cur).wait()
    _drain(pend_ref[cur], zero_ref, xs_ref, sem_out.at[cur])

    hfp = hf_ref[prv]
    lt = lax.dot_general(wrt_ref[...], hfp, (((1,), (1,)), ((), ())),
                         preferred_element_type=jnp.float32)
    e0, e1, gate0, gate1 = _route(lt)
    esub = lax.broadcasted_iota(jnp.int32, (N_EXPERTS, tm), 0)
    dead = (s == 0).astype(jnp.int32)
    oh0 = esub == e0 - dead * N_EXPERTS
    oh1 = esub == e1 - dead * N_EXPERTS
    oh = jnp.where(oh0, 1.0, jnp.where(oh1, 1.0, 0.0))
    ct = jnp.dot(oh.astype(jnp.bfloat16), tri_ref[...], preferred_element_type=jnp.float32)
    n_col = jnp.sum(oh, axis=1, keepdims=True)
    nch_col = jnp.floor((n_col + (CHUNK - 1)) * (1.0 / CHUNK))
    e_sub = lax.broadcasted_iota(jnp.int32, (N_EXPERTS, LANES), 0)
    l_lane = lax.broadcasted_iota(jnp.int32, (N_EXPERTS, LANES), 1)
    nch_row = jnp.sum(jnp.where(e_sub == l_lane, nch_col, 0.0), axis=0, keepdims=True)
    coff_col = jnp.sum(jnp.where(l_lane < e_sub, nch_row, 0.0), axis=1, keepdims=True)
    coff_row = jnp.sum(jnp.where(e_sub == l_lane, coff_col, 0.0), axis=0, keepdims=True)
    info_v_ref[...] = jnp.concatenate(
        [nch_row, coff_row, jnp.zeros((SUBLANES - 2, LANES), jnp.float32)], axis=0
    ).astype(jnp.int32)
    info_copy(prv).start()

    posbase = ct + coff_col * CHUNK
    pos0 = jnp.sum(jnp.where(oh0, posbase, 0.0), axis=0, keepdims=True)
    pos1 = jnp.sum(jnp.where(oh1, posbase, 0.0), axis=0, keepdims=True)
    route_t = jnp.concatenate(
        [pos0, pos1, gate0, gate1, jnp.zeros((LANES - 4, tm), jnp.float32)], axis=0)
    route_ref[...] = route_t.T

    pos0i = pos0.astype(jnp.int32)
    pos1i = pos1.astype(jnp.int32)
    for r0 in range(0, STAGE_ROWS, PERM_ROWS):
        r = lax.broadcasted_iota(jnp.int32, (PERM_ROWS, tm), 0) + r0
        p = jnp.where(r == pos0i, 1.0, jnp.where(r == pos1i, 1.0, 0.0)).astype(jnp.bfloat16)
        rows = jnp.dot(p, hfp, preferred_element_type=jnp.float32)
        stage_ref[st_sort, r0 // CHUNK:(r0 + PERM_ROWS) // CHUNK] = (
            rows.astype(jnp.bfloat16).reshape(PERM_ROWS // CHUNK, CHUNK, x_ref.shape[-1]))

    x = x_ref[...]
    hn = _rms(x, nmix).astype(jnp.bfloat16)
    proj = jnp.dot(hn, win_ref[...], preferred_element_type=jnp.float32)
    b_gate = proj[:, 0:d_conv]
    v_pool = proj[:, 3 * d_conv:]
    extu_ref[HALO:, :] = proj[:, d_conv:2 * d_conv] * proj[:, 2 * d_conv:3 * d_conv]
    extp_ref[HALO:, :] = v_pool

    conv = convw_ref[CONV_WIDTH - 1:CONV_WIDTH, :] * extu_ref[HALO:, :]
    for k in range(CONV_WIDTH - 1):
        off = HALO - (CONV_WIDTH - 1 - k)
        conv = conv + convw_ref[k:k + 1, :] * extu_ref[off:off + tm, :]
    y_conv = _rms(b_gate * conv, nconv_ref[...])

    mixed = []
    for g, w in enumerate(POOL_WINDOWS):
        cols = slice(g * LANES, (g + 1) * LANES)
        acc = extp_ref[HALO:, cols]
        for j in range(1, w):
            acc = acc + extp_ref[HALO - j:HALO - j + tm, cols]
        pooled = acc * (1.0 / w) - v_pool[:, cols]
        mixed.append(jnp.dot(pooled.astype(jnp.bfloat16), poolw_ref[g],
                             preferred_element_type=jnp.float32))
    y_pool = _rms(jnp.concatenate(mixed, axis=-1) * pscale_ref[...], npool_ref[...])

    y = jnp.concatenate([y_conv, y_pool], axis=-1).astype(jnp.bfloat16)
    h1 = x + jnp.dot(y, wout_ref[...], preferred_element_type=jnp.float32)
    h1_ref[...] = h1
    hf_ref[cur] = _rms(h1, nffn_ref[...]).astype(jnp.bfloat16)

    extu_ref[0:HALO, :] = extu_ref[tm:tm + HALO, :]
    extp_ref[0:HALO, :] = extp_ref[tm:tm + HALO, :]

    tile2 = jnp.maximum(s - 2, 0)
    tot = 0
    most = 0
    for e in range(N_EXPERTS):
        nch = info_s_ref[cur, 0, e]
        base = run_ref[e]
        tab_ref[tile2 * TAB_STRIDE + e] = base
        tab_ref[tile2 * TAB_STRIDE + N_EXPERTS + e] = nch
        _issue_head(nch, stage_ref.at[st_disp], info_s_ref[cur, 1, e],
                    xs_ref, e * cap_chunks + base, sem_out.at[cur])
        run_ref[e] = base + nch
        tot = tot + nch
        most = jnp.maximum(most, nch)
    tab_ref[tile2 * TAB_STRIDE + 2 * N_EXPERTS] = tot
    pend_ref[cur] = tot

    @pl.when(most > UNROLLED_CHUNKS)
    def _():
        def body(e, c):
            _issue_rest(UNROLLED_CHUNKS, info_s_ref[cur, 0, e],
                        stage_ref.at[st_disp], info_s_ref[cur, 1, e],
                        xs_ref, e * cap_chunks + tab_ref[tile2 * TAB_STRIDE + e], sem_out.at[cur])
            return c
        lax.fori_loop(0, N_EXPERTS, body, 0)

    @pl.when(s == n_tiles)
    def _():
        info_copy(prv).wait()
        _drain(pend_ref[prv], zero_ref, xs_ref, sem_out.at[prv])
        _drain(pend_ref[cur], zero_ref, xs_ref, sem_out.at[cur])
        sem_last = sem_out.at[0]

        def last_body(e, tot):
            nch = info_s_ref[prv, 0, e]
            base = run_ref[e]
            tab_ref[(s - 1) * TAB_STRIDE + e] = base
            tab_ref[(s - 1) * TAB_STRIDE + N_EXPERTS + e] = nch
            _issue_rest(0, nch, stage_ref.at[st_sort], info_s_ref[prv, 1, e],
                        xs_ref, e * cap_chunks + base, sem_last)
            run_ref[e] = base + nch
            return tot + nch
        last = lax.fori_loop(0, N_EXPERTS, last_body, 0)
        tab_ref[(s - 1) * TAB_STRIDE + 2 * N_EXPERTS] = last
        _drain(last, zero_ref, xs_ref, sem_last)

        def pad_body(e, carry):
            tot, w = carry
            n = run_ref[e]
            tiles = lax.shift_right_logical(n + (MOE_CHUNKS - 1), MOE_CHUNKS_LOG2)
            nch = tiles * MOE_CHUNKS - n

            def chunk_body(k, c):
                _copy_chunks(zero_ref, 0, xs_ref, e * cap_chunks + n + k, sem_last).start()
                return c
            lax.fori_loop(0, nch, chunk_body, 0)

            def work_body(j, c):
                wexp_ref[w + j] = e
                wblk_ref[w + j] = e * (cap_chunks // MOE_CHUNKS) + j
                return c
            lax.fori_loop(0, tiles, work_body, 0)
            return tot + nch, w + tiles
        n_pad, n_work = lax.fori_loop(0, N_EXPERTS, pad_body, (0, 0))
        _drain(n_pad, zero_ref, xs_ref, sem_last)
        nwork_ref[0] = n_work

        def fill_body(w, c):
            wexp_ref[w] = wexp_ref[n_work - 1]
            wblk_ref[w] = wblk_ref[n_work - 1]
            return c
        lax.fori_loop(n_work, max_work, fill_body, 0)


def _moe_kernel(exp_ref, blk_ref, n_ref, xs_ref, wg_ref, wu_ref, wd_ref, ys_ref,
                wgu_bf_ref, wd_bf_ref, *, d_exp):
    w = pl.program_id(0)

    @pl.when((w == 0) | (exp_ref[w] != exp_ref[jnp.maximum(w - 1, 0)]))
    def _():
        wgu_bf_ref[:, :d_exp] = wg_ref[0].astype(jnp.bfloat16)
        wgu_bf_ref[:, d_exp:] = wu_ref[0].astype(jnp.bfloat16)
        wd_bf_ref[...] = wd_ref[0].astype(jnp.bfloat16)

    @pl.when(w < n_ref[0])
    def _():
        gu = jnp.dot(xs_ref[...], wgu_bf_ref[...], preferred_element_type=jnp.float32)
        g = gu[:, :d_exp]
        a = g * jax.nn.sigmoid(g) * gu[:, d_exp:]
        ys_ref[...] = jnp.dot(a.astype(jnp.bfloat16), wd_bf_ref[...],
                              preferred_element_type=jnp.float32).astype(jnp.bfloat16)


def _combine_kernel(tab_ref, h1_ref, route_ref, fn_ref, ys_ref, out_ref, stage_ref, sem,
                    *, tm, cap_chunks):
    s = pl.program_id(0)
    n_steps = pl.num_programs(0)
    slot = s % 3
    ahead = (s + 2) % 3

    @pl.when(s == 0)
    def _():
        stage_ref[...] = jnp.zeros_like(stage_ref)
        for j in range(2):
            def body(e, coff):
                nch = tab_ref[j * TAB_STRIDE + N_EXPERTS + e]
                _issue_rest(0, nch, ys_ref, e * cap_chunks + tab_ref[j * TAB_STRIDE + e],
                            stage_ref.at[j], coff, sem.at[j])
                return coff + nch
            lax.fori_loop(0, N_EXPERTS, body, 0)

    _drain(tab_ref[s * TAB_STRIDE + 2 * N_EXPERTS], ys_ref, stage_ref.at[slot], sem.at[slot])

    route = route_ref[...]
    pos0 = route[:, 0:1].astype(jnp.int32)
    pos1 = route[:, 1:2].astype(jnp.int32)
    gate0 = route[:, 2:3]
    gate1 = route[:, 3:4]
    r = lax.broadcasted_iota(jnp.int32, (tm, STAGE_ROWS), 1)
    pw = jnp.where(r == pos0, gate0, jnp.where(r == pos1, gate1, 0.0)).astype(jnp.bfloat16)
    moe = jnp.dot(pw, stage_ref[slot].reshape(STAGE_ROWS, out_ref.shape[-1]),
                  preferred_element_type=jnp.float32)
    out_ref[...] = _rms(h1_ref[...] + moe, fn_ref[...])

    nxt = jnp.minimum(s + 2, n_steps - 1)
    has_next = (s + 2 < n_steps).astype(jnp.int32)
    coff = 0
    most = 0
    for e in range(N_EXPERTS):
        nch = tab_ref[nxt * TAB_STRIDE + N_EXPERTS + e] * has_next
        _issue_head(nch, ys_ref, e * cap_chunks + tab_ref[nxt * TAB_STRIDE + e],
                    stage_ref.at[ahead], coff, sem.at[ahead])
        coff = coff + nch
        most = jnp.maximum(most, nch)

    @pl.when(most > UNROLLED_CHUNKS)
    def _():
        def body(e, coff):
            nch = tab_ref[nxt * TAB_STRIDE + N_EXPERTS + e]
            _issue_rest(UNROLLED_CHUNKS, nch, ys_ref, e * cap_chunks + tab_ref[nxt * TAB_STRIDE + e],
                        stage_ref.at[ahead], coff, sem.at[ahead])
            return coff + nch
        lax.fori_loop(0, N_EXPERTS, body, 0)


def kernel(x, meta_tokens, norm_mix, w_in, conv_w, norm_conv_out, pool_w, pool_scale,
           norm_pool_out, w_out, norm_ffn, w_router_group, w_router_expert,
           w_gate, w_up, w_down, final_norm):
    bsz, seq, d = x.shape
    d_conv = conv_w.shape[-1]
    d_exp = w_gate.shape[-1]
    n_exp = w_gate.shape[1]
    assert meta_tokens.shape[0] == N_META == HALO and n_exp == N_EXPERTS
    assert seq % MIX_TILE == 0
    bf = jnp.bfloat16
    i32 = jnp.int32
    t = bsz * seq
    n_tiles = t // MIX_TILE
    cap = -(-(t + CHUNK * n_tiles) // MOE_TILE) * MOE_TILE + MOE_TILE
    n_chunks = N_EXPERTS * cap // CHUNK
    max_work = (2 * t + N_EXPERTS * (CHUNK - 1) * n_tiles) // MOE_TILE + N_EXPERTS
    row = lambda a: a.reshape(1, -1)
    wrt = jnp.concatenate(
        [w_router_expert[0].T, w_router_group[0].T,
         jnp.zeros((ROUTER_ROWS - N_EXPERTS - N_EXPERT_GROUPS, d), jnp.float32)], axis=0).astype(bf)
    tri = (lax.broadcasted_iota(i32, (MIX_TILE, MIX_TILE), 0)
           < lax.broadcasted_iota(i32, (MIX_TILE, MIX_TILE), 1)).astype(bf)

    full = lambda shape: pl.BlockSpec(shape, lambda s: (0,) * len(shape))
    last = n_tiles - 1
    smem = pl.BlockSpec(memory_space=pltpu.SMEM)
    h1, route, xs, tab, wexp, wblk, nwork = pl.pallas_call(
        functools.partial(_mixer_kernel, tm=MIX_TILE, d_conv=d_conv, cap_chunks=cap // CHUNK,
                          tiles_per_seq=seq // MIX_TILE, max_work=max_work),
        grid=(n_tiles + 1,),
        in_specs=[pl.BlockSpec((MIX_TILE, d), lambda s: (jnp.minimum(s, last), 0)),
                  full((N_META, d)), full((1, d)), full((d, 4 * d_conv)),
                  full((CONV_WIDTH, d_conv)), full((1, d_conv)),
                  full((len(POOL_WINDOWS), LANES, LANES)), full((1, d_conv)), full((1, d_conv)),
                  full((2 * d_conv, d)), full((1, d)), full((ROUTER_ROWS, d)),
                  full((MIX_TILE, MIX_TILE))],
        out_specs=[pl.BlockSpec((MIX_TILE, d), lambda s: (s, 0)),
                   pl.BlockSpec((MIX_TILE, LANES), lambda s: (jnp.maximum(s - 1, 0), 0)),
                   pl.BlockSpec(memory_space=pl.ANY), smem, smem, smem, smem],
        out_shape=[jax.ShapeDtypeStruct((t + MIX_TILE, d), jnp.float32),
                   jax.ShapeDtypeStruct((t, LANES), jnp.float32),
                   jax.ShapeDtypeStruct((n_chunks, CHUNK, d), bf),
                   jax.ShapeDtypeStruct((n_tiles * TAB_STRIDE,), i32),
                   jax.ShapeDtypeStruct((max_work,), i32),
                   jax.ShapeDtypeStruct((max_work,), i32),
                   jax.ShapeDtypeStruct((1,), i32)],
        scratch_shapes=[pltpu.VMEM((MIX_TILE + HALO, d_conv), jnp.float32),
                        pltpu.VMEM((MIX_TILE + HALO, d_conv), jnp.float32),
                        pltpu.VMEM((2, MIX_TILE, d), bf),
                        pltpu.VMEM((3, STAGE_CHUNKS, CHUNK, d), bf),
                        pltpu.VMEM((WAIT_GROUP, CHUNK, d), bf),
                        pltpu.VMEM((SUBLANES, LANES), i32),
                        pltpu.SMEM((2, SUBLANES, LANES), i32),
                        pltpu.SMEM((N_EXPERTS,), i32),
                        pltpu.SMEM((2,), i32),
                        pltpu.SemaphoreType.DMA,
                        pltpu.SemaphoreType.DMA((2,))],
        compiler_params=pltpu.CompilerParams(
            dimension_semantics=("arbitrary",), vmem_limit_bytes=VMEM_LIMIT),
        name="mixer",
    )(x.reshape(t, d), meta_tokens, row(norm_mix), w_in[0].astype(bf), conv_w[0],
      row(norm_conv_out), pool_w[0].astype(bf), row(pool_scale), row(norm_pool_out),
      w_out[0].astype(bf), row(norm_ffn), wrt, tri)

    ys = pl.pallas_call(
        functools.partial(_moe_kernel, d_exp=d_exp),
        grid_spec=pltpu.PrefetchScalarGridSpec(
            num_scalar_prefetch=3,
            grid=(max_work,),
            in_specs=[pl.BlockSpec((MOE_TILE, d), lambda w, ex, bk, n: (bk[w], 0)),
                      pl.BlockSpec((1, d, d_exp), lambda w, ex, bk, n: (ex[w], 0, 0)),
                      pl.BlockSpec((1, d, d_exp), lambda w, ex, bk, n: (ex[w], 0, 0)),
                      pl.BlockSpec((1, d_exp, d), lambda w, ex, bk, n: (ex[w], 0, 0))],
            out_specs=pl.BlockSpec((MOE_TILE, d), lambda w, ex, bk, n: (bk[w], 0)),
            scratch_shapes=[pltpu.VMEM((d, 2 * d_exp), bf), pltpu.VMEM((d_exp, d), bf)]),
        out_shape=jax.ShapeDtypeStruct((n_chunks * CHUNK, d), bf),
        compiler_params=pltpu.CompilerParams(
            dimension_semantics=("arbitrary",), vmem_limit_bytes=VMEM_LIMIT),
        name="moe",
    )(wexp, wblk, nwork, xs.reshape(n_chunks * CHUNK, d), w_gate[0], w_up[0], w_down[0])

    out = pl.pallas_call(
        functools.partial(_combine_kernel, tm=MIX_TILE, cap_chunks=cap // CHUNK),
        grid_spec=pltpu.PrefetchScalarGridSpec(
            num_scalar_prefetch=1,
            grid=(n_tiles,),
            in_specs=[pl.BlockSpec((MIX_TILE, d), lambda s, tb: (s, 0)),
                      pl.BlockSpec((MIX_TILE, LANES), lambda s, tb: (s, 0)),
                      pl.BlockSpec((1, d), lambda s, tb: (0, 0)),
                      pl.BlockSpec(memory_space=pl.ANY)],
            out_specs=pl.BlockSpec((MIX_TILE, d), lambda s, tb: (s, 0)),
            scratch_shapes=[pltpu.VMEM((3, STAGE_CHUNKS, CHUNK, d), bf),
                            pltpu.SemaphoreType.DMA((3,))]),
        out_shape=jax.ShapeDtypeStruct((t, d), jnp.float32),
        compiler_params=pltpu.CompilerParams(
            dimension_semantics=("arbitrary",), vmem_limit_bytes=VMEM_LIMIT),
        name="combine",
    )(tab, h1, route, row(final_norm), ys.reshape(n_chunks, CHUNK, d))
    return out.reshape(bsz, seq, d)
```

```python
import functools

import jax
import jax.numpy as jnp
from jax import lax
from jax.experimental import pallas as pl
from jax.experimental.pallas import tpu as pltpu

EPS = 1e-6
N_META = 16
CONV_WIDTH = 3
POOL_WINDOWS = (2, 4, 8, 16)
N_EXPERT_GROUPS = 4
EXPERTS_PER_GROUP = 8
N_EXPERTS = N_EXPERT_GROUPS * EXPERTS_PER_GROUP
LANES = 128
SUBLANES = 8
HALO = 16
CHUNK = 16
ROUTER_ROWS = 64

MIX_TILE = 512
MOE_TILE = 1024
MOE_CHUNKS = MOE_TILE // CHUNK
MOE_CHUNKS_LOG2 = MOE_CHUNKS.bit_length() - 1
VMEM_LIMIT = 56 * 1024 * 1024

STAGE_ROWS = 2 * MIX_TILE + N_EXPERTS * CHUNK
STAGE_CHUNKS = STAGE_ROWS // CHUNK
PERM_ROWS = 512
TAB_STRIDE = 2 * N_EXPERTS + 1
UNROLLED_CHUNKS = 5
SEGMENT_CHUNKS = 5
WAIT_GROUP = 16
WAIT_GROUP_LOG2 = WAIT_GROUP.bit_length() - 1


def _rms(x, g):
    return x * lax.rsqrt(jnp.mean(x * x, axis=-1, keepdims=True) + EPS) * g


def _route(lt):
    tm = lt.shape[1]
    sub = lax.broadcasted_iota(jnp.int32, (SUBLANES, tm), 0)
    neg = jnp.float32(-jnp.inf)
    big = jnp.int32(1 << 20)
    lg = jnp.where(sub < N_EXPERT_GROUPS, lt[N_EXPERTS:N_EXPERTS + SUBLANES], neg)
    gmax = jnp.max(lg, axis=0, keepdims=True)
    g1 = 1.0 / jnp.sum(jnp.exp(lg - gmax), axis=0, keepdims=True)
    gsel = jnp.min(jnp.where(lg == gmax, sub, big), axis=0, keepdims=True)
    le = lt[(N_EXPERT_GROUPS - 1) * SUBLANES:N_EXPERT_GROUPS * SUBLANES]
    for g in range(N_EXPERT_GROUPS - 2, -1, -1):
        le = jnp.where(gsel == g, lt[g * SUBLANES:(g + 1) * SUBLANES], le)
    m1 = jnp.max(le, axis=0, keepdims=True)
    pe = jnp.exp(le - m1)
    i1 = jnp.min(jnp.where(le == m1, sub, big), axis=0, keepdims=True)
    pe2 = jnp.where(sub == i1, -1.0, pe)
    p2 = jnp.max(pe2, axis=0, keepdims=True)
    i2 = jnp.min(jnp.where(pe2 == p2, sub, big), axis=0, keepdims=True)
    inv = 1.0 / (1.0 + p2)
    return (gsel * EXPERTS_PER_GROUP + i1, gsel * EXPERTS_PER_GROUP + i2,
            g1 * inv, g1 * p2 * inv)


def _copy_chunks(src_ref, src_chunk, dst_ref, dst_chunk, sem, n=1):
    return pltpu.make_async_copy(src_ref.at[pl.ds(src_chunk, n)],
                                 dst_ref.at[pl.ds(dst_chunk, n)], sem)


def _drain(n_chunks, src_ref, dst_ref, sem):
    def group_body(k, c):
        _copy_chunks(src_ref, 0, dst_ref, 0, sem, WAIT_GROUP).wait()
        return c
    lax.fori_loop(0, lax.shift_right_logical(n_chunks, WAIT_GROUP_LOG2), group_body, 0)

    def one_body(k, c):
        _copy_chunks(src_ref, 0, dst_ref, 0, sem).wait()
        return c
    lax.fori_loop(0, n_chunks & (WAIT_GROUP - 1), one_body, 0)


def _issue_head(nch, src_ref, src0, dst_ref, dst0, sem):
    for k in range(UNROLLED_CHUNKS):
        @pl.when(k < nch)
        def _():
            _copy_chunks(src_ref, src0 + k, dst_ref, dst0 + k, sem).start()


def _issue_rest(first, nch, src_ref, src0, dst_ref, dst0, sem):
    def body(k, c):
        _copy_chunks(src_ref, src0 + k, dst_ref, dst0 + k, sem).start()
        return c
    lax.fori_loop(first, nch, body, 0)


def _mixer_kernel(x_ref, meta_ref, nmix_ref, win_ref, convw_ref, nconv_ref, poolw_ref,
                  pscale_ref, npool_ref, wout_ref, nffn_ref, wrt_ref, tri_ref,
                  h1_ref, route_ref, xs_ref, tab_ref, wexp_ref, wblk_ref, nwork_ref,
                  extu_ref, extp_ref, hf_ref, stage_ref, zero_ref, info_v_ref, info_s_ref,
                  run_ref, pend_ref, sem_info, sem_out,
                  *, tm, d_conv, cap_chunks, tiles_per_seq, max_work):
    s = pl.program_id(0)
    n_tiles = pl.num_programs(0) - 1
    cur = s % 2
    prv = 1 - cur
    st_sort = (s + 2) % 3
    st_disp = (s + 1) % 3
    nmix = nmix_ref[...]

    def info_copy(slot):
        return pltpu.make_async_copy(info_v_ref, info_s_ref.at[slot], sem_info)

    @pl.when(s == 0)
    def _():
        pend_ref[0] = 0
        zero_ref[...] = jnp.zeros_like(zero_ref)
        stage_ref[...] = jnp.zeros_like(stage_ref)
        hf_ref[...] = jnp.zeros_like(hf_ref)
        info_v_ref[...] = jnp.zeros_like(info_v_ref)
        info_copy(0).start()
        info_copy(0).wait()
        info_copy(1).start()

        def init(e, c):
            run_ref[e] = 0
            return c
        lax.fori_loop(0, N_EXPERTS, init, 0)

    @pl.when(s % tiles_per_seq == 0)
    def _():
        hm = _rms(meta_ref[...], nmix).astype(jnp.bfloat16)
        pm = jnp.dot(hm, win_ref[:, d_conv:], preferred_element_type=jnp.float32)
        extu_ref[0:HALO, :] = pm[:, 0:d_conv] * pm[:, d_conv:2 * d_conv]
        extp_ref[0:HALO, :] = pm[:, 2 * d_conv:]

    info_copy(cur).wait()

    hfp = hf_ref[prv]
    lt = lax.dot_general(wrt_ref[...], hfp, (((1,), (1,)), ((), ())),
                         preferred_element_type=jnp.float32)
    e0, e1, gate0, gate1 = _route(lt)
    esub = lax.broadcasted_iota(jnp.int32, (N_EXPERTS, tm), 0)
    dead = (s == 0).astype(jnp.int32)
    oh0 = esub == e0 - dead * N_EXPERTS
    oh1 = esub == e1 - dead * N_EXPERTS
    oh = jnp.where(oh0, 1.0, jnp.where(oh1, 1.0, 0.0))
    ct = jnp.dot(oh.astype(jnp.bfloat16), tri_ref[...], preferred_element_type=jnp.float32)
    n_col = jnp.sum(oh, axis=1, keepdims=True)
    nch_col = jnp.floor((n_col + (CHUNK - 1)) * (1.0 / CHUNK))
    e_sub = lax.broadcasted_iota(jnp.int32, (N_EXPERTS, LANES), 0)
    l_lane = lax.broadcasted_iota(jnp.int32, (N_EXPERTS, LANES), 1)
    nch_row = jnp.sum(jnp.where(e_sub == l_lane, nch_col, 0.0), axis=0, keepdims=True)
    coff_col = jnp.sum(jnp.where(l_lane < e_sub, nch_row, 0.0), axis=1, keepdims=True)
    coff_row = jnp.sum(jnp.where(e_sub == l_lane, coff_col, 0.0), axis=0, keepdims=True)
    info_v_ref[...] = jnp.concatenate(
        [nch_row, coff_row, jnp.zeros((SUBLANES - 2, LANES), jnp.float32)], axis=0
    ).astype(jnp.int32)
    info_copy(prv).start()

    posbase = ct + coff_col * CHUNK
    pos0 = jnp.sum(jnp.where(oh0, posbase, 0.0), axis=0, keepdims=True)
    pos1 = jnp.sum(jnp.where(oh1, posbase, 0.0), axis=0, keepdims=True)
    route_t = jnp.concatenate(
        [pos0, pos1, gate0, gate1, jnp.zeros((LANES - 4, tm), jnp.float32)], axis=0)
    route_ref[...] = route_t.T

    pos0i = pos0.astype(jnp.int32)
    pos1i = pos1.astype(jnp.int32)
    for r0 in range(0, STAGE_ROWS, PERM_ROWS):
        r = lax.broadcasted_iota(jnp.int32, (PERM_ROWS, tm), 0) + r0
        p = jnp.where(r == pos0i, 1.0, jnp.where(r == pos1i, 1.0, 0.0)).astype(jnp.bfloat16)
        rows = jnp.dot(p, hfp, preferred_element_type=jnp.float32)
        stage_ref[st_sort, r0 // CHUNK:(r0 + PERM_ROWS) // CHUNK] = (
            rows.astype(jnp.bfloat16).reshape(PERM_ROWS // CHUNK, CHUNK, x_ref.shape[-1]))

    x = x_ref[...]
    hn = _rms(x, nmix).astype(jnp.bfloat16)
    proj = jnp.dot(hn, win_ref[...], preferred_element_type=jnp.float32)
    b_gate = proj[:, 0:d_conv]
    v_pool = proj[:, 3 * d_conv:]
    extu_ref[HALO:, :] = proj[:, d_conv:2 * d_conv] * proj[:, 2 * d_conv:3 * d_conv]
    extp_ref[HALO:, :] = v_pool

    conv = convw_ref[CONV_WIDTH - 1:CONV_WIDTH, :] * extu_ref[HALO:, :]
    for k in range(CONV_WIDTH - 1):
        off = HALO - (CONV_WIDTH - 1 - k)
        conv = conv + convw_ref[k:k + 1, :] * extu_ref[off:off + tm, :]
    y_conv = _rms(b_gate * conv, nconv_ref[...])

    mixed = []
    for g, w in enumerate(POOL_WINDOWS):
        cols = slice(g * LANES, (g + 1) * LANES)
        acc = extp_ref[HALO:, cols]
        for j in range(1, w):
            acc = acc + extp_ref[HALO - j:HALO - j + tm, cols]
        pooled = acc * (1.0 / w) - v_pool[:, cols]
        mixed.append(jnp.dot(pooled.astype(jnp.bfloat16), poolw_ref[g],
                             preferred_element_type=jnp.float32))
    y_pool = _rms(jnp.concatenate(mixed, axis=-1) * pscale_ref[...], npool_ref[...])

    y = jnp.concatenate([y_conv, y_pool], axis=-1).astype(jnp.bfloat16)
    h1 = x + jnp.dot(y, wout_ref[...], preferred_element_type=jnp.float32)
    h1_ref[...] = h1
    hf_ref[cur] = _rms(h1, nffn_ref[...]).astype(jnp.bfloat16)

    extu_ref[0:HALO, :] = extu_ref[tm:tm + HALO, :]
    extp_ref[0:HALO, :] = extp_ref[tm:tm + HALO, :]

    pending = pend_ref[0]
    tile2 = jnp.maximum(s - 2, 0)
    tot = 0
    issued = 0
    most = 0
    plan = []
    for e in range(N_EXPERTS):
        nch = info_s_ref[cur, 0, e]
        base = run_ref[e]
        tab_ref[tile2 * TAB_STRIDE + e] = base
        tab_ref[tile2 * TAB_STRIDE + N_EXPERTS + e] = nch
        run_ref[e] = base + nch
        plan.append((nch > 0, info_s_ref[cur, 1, e], e * cap_chunks + base))
        tot = tot + nch
        issued = issued + jnp.where(nch > 0, SEGMENT_CHUNKS, 0) + jnp.maximum(nch - SEGMENT_CHUNKS, 0)
        most = jnp.maximum(most, nch)
    tab_ref[tile2 * TAB_STRIDE + 2 * N_EXPERTS] = tot
    pend_ref[0] = issued
    _drain(pending, zero_ref, xs_ref, sem_out)
    for nonempty, src0, dst0 in plan:
        @pl.when(nonempty)
        def _():
            _copy_chunks(stage_ref.at[st_disp], src0, xs_ref, dst0, sem_out, SEGMENT_CHUNKS).start()

    @pl.when(most > SEGMENT_CHUNKS)
    def _():
        def body(e, c):
            _issue_rest(SEGMENT_CHUNKS, info_s_ref[cur, 0, e],
                        stage_ref.at[st_disp], info_s_ref[cur, 1, e],
                        xs_ref, e * cap_chunks + tab_ref[tile2 * TAB_STRIDE + e], sem_out)
            return c
        lax.fori_loop(0, N_EXPERTS, body, 0)

    @pl.when(s == n_tiles)
    def _():
        info_copy(prv).wait()
        _drain(pend_ref[0], zero_ref, xs_ref, sem_out)
        sem_last = sem_out

        def last_body(e, tot):
            nch = info_s_ref[prv, 0, e]
            base = run_ref[e]
            tab_ref[(s - 1) * TAB_STRIDE + e] = base
            tab_ref[(s - 1) * TAB_STRIDE + N_EXPERTS + e] = nch
            _issue_rest(0, nch, stage_ref.at[st_sort], info_s_ref[prv, 1, e],
                        xs_ref, e * cap_chunks + base, sem_last)
            run_ref[e] = base + nch
            return tot + nch
        last = lax.fori_loop(0, N_EXPERTS, last_body, 0)
        tab_ref[(s - 1) * TAB_STRIDE + 2 * N_EXPERTS] = last
        _drain(last, zero_ref, xs_ref, sem_last)

        def pad_body(e, carry):
            tot, w = carry
            n = run_ref[e]
            tiles = lax.shift_right_logical(n + (MOE_CHUNKS - 1), MOE_CHUNKS_LOG2)
            nch = tiles * MOE_CHUNKS - n

            def chunk_body(k, c):
                _copy_chunks(zero_ref, 0, xs_ref, e * cap_chunks + n + k, sem_last).start()
                return c
            lax.fori_loop(0, nch, chunk_body, 0)

            def work_body(j, c):
                wexp_ref[w + j] = e
                wblk_ref[w + j] = e * (cap_chunks // MOE_CHUNKS) + j
                return c
            lax.fori_loop(0, tiles, work_body, 0)
            return tot + nch, w + tiles
        n_pad, n_work = lax.fori_loop(0, N_EXPERTS, pad_body, (0, 0))
        _drain(n_pad, zero_ref, xs_ref, sem_last)
        nwork_ref[0] = n_work

        def fill_body(w, c):
            wexp_ref[w] = wexp_ref[n_work - 1]
            wblk_ref[w] = wblk_ref[n_work - 1]
            return c
        lax.fori_loop(n_work, max_work, fill_body, 0)


def _moe_kernel(exp_ref, blk_ref, n_ref, xs_ref, wg_ref, wu_ref, wd_ref, ys_ref,
                wgu_bf_ref, wd_bf_ref, *, d_exp):
    w = pl.program_id(0)

    @pl.when((w == 0) | (exp_ref[w] != exp_ref[jnp.maximum(w - 1, 0)]))
    def _():
        wgu_bf_ref[:, :d_exp] = wg_ref[0].astype(jnp.bfloat16)
        wgu_bf_ref[:, d_exp:] = wu_ref[0].astype(jnp.bfloat16)
        wd_bf_ref[...] = wd_ref[0].astype(jnp.bfloat16)

    @pl.when(w < n_ref[0])
    def _():
        gu = jnp.dot(xs_ref[...], wgu_bf_ref[...], preferred_element_type=jnp.float32)
        g = gu[:, :d_exp]
        a = g * jax.nn.sigmoid(g) * gu[:, d_exp:]
        ys_ref[...] = jnp.dot(a.astype(jnp.bfloat16), wd_bf_ref[...],
                              preferred_element_type=jnp.float32).astype(jnp.bfloat16)


def _combine_kernel(tab_ref, h1_ref, route_ref, fn_ref, ys_ref, out_ref, stage_ref, sem,
                    *, tm, cap_chunks):
    s = pl.program_id(0)
    n_steps = pl.num_programs(0)
    slot = s % 3
    ahead = (s + 2) % 3

    @pl.when(s == 0)
    def _():
        stage_ref[...] = jnp.zeros_like(stage_ref)
        for j in range(2):
            def body(e, coff):
                nch = tab_ref[j * TAB_STRIDE + N_EXPERTS + e]
                _issue_rest(0, nch, ys_ref, e * cap_chunks + tab_ref[j * TAB_STRIDE + e],
                            stage_ref.at[j], coff, sem.at[j])
                return coff + nch
            lax.fori_loop(0, N_EXPERTS, body, 0)

    _drain(tab_ref[s * TAB_STRIDE + 2 * N_EXPERTS], ys_ref, stage_ref.at[slot], sem.at[slot])

    route = route_ref[...]
    pos0 = route[:, 0:1].astype(jnp.int32)
    pos1 = route[:, 1:2].astype(jnp.int32)
    gate0 = route[:, 2:3]
    gate1 = route[:, 3:4]
    r = lax.broadcasted_iota(jnp.int32, (tm, STAGE_ROWS), 1)
    pw = jnp.where(r == pos0, gate0, jnp.where(r == pos1, gate1, 0.0)).astype(jnp.bfloat16)
    moe = jnp.dot(pw, stage_ref[slot].reshape(STAGE_ROWS, out_ref.shape[-1]),
                  preferred_element_type=jnp.float32)
    out_ref[...] = _rms(h1_ref[...] + moe, fn_ref[...])

    nxt = jnp.minimum(s + 2, n_steps - 1)
    has_next = (s + 2 < n_steps).astype(jnp.int32)
    coff = 0
    most = 0
    for e in range(N_EXPERTS):
        nch = tab_ref[nxt * TAB_STRIDE + N_EXPERTS + e] * has_next
        _issue_head(nch, ys_ref, e * cap_chunks + tab_ref[nxt * TAB_STRIDE + e],
                    stage_ref.at[ahead], coff, sem.at[ahead])
        coff = coff + nch
        most = jnp.maximum(most, nch)

    @pl.when(most > UNROLLED_CHUNKS)
    def _():
        def body(e, coff):
            nch = tab_ref[nxt * TAB_STRIDE + N_EXPERTS + e]
            _issue_rest(UNROLLED_CHUNKS, nch, ys_ref, e * cap_chunks + tab_ref[nxt * TAB_STRIDE + e],
                        stage_ref.at[ahead], coff, sem.at[ahead])
            return coff + nch
        lax.fori_loop(0, N_EXPERTS, body, 0)


def kernel(x, meta_tokens, norm_mix, w_in, conv_w, norm_conv_out, pool_w, pool_scale,
           norm_pool_out, w_out, norm_ffn, w_router_group, w_router_expert,
           w_gate, w_up, w_down, final_norm):
    bsz, seq, d = x.shape
    d_conv = conv_w.shape[-1]
    d_exp = w_gate.shape[-1]
    n_exp = w_gate.shape[1]
    assert meta_tokens.shape[0] == N_META == HALO and n_exp == N_EXPERTS
    assert seq % MIX_TILE == 0
    bf = jnp.bfloat16
    i32 = jnp.int32
    t = bsz * seq
    n_tiles = t // MIX_TILE
    cap = -(-(t + CHUNK * n_tiles) // MOE_TILE) * MOE_TILE + MOE_TILE
    n_chunks = N_EXPERTS * cap // CHUNK
    max_work = (2 * t + N_EXPERTS * (CHUNK - 1) * n_tiles) // MOE_TILE + N_EXPERTS
    row = lambda a: a.reshape(1, -1)
    wrt = jnp.concatenate(
        [w_router_expert[0].T, w_router_group[0].T,
         jnp.zeros((ROUTER_ROWS - N_EXPERTS - N_EXPERT_GROUPS, d), jnp.float32)], axis=0).astype(bf)
    tri = (lax.broadcasted_iota(i32, (MIX_TILE, MIX_TILE), 0)
           < lax.broadcasted_iota(i32, (MIX_TILE, MIX_TILE), 1)).astype(bf)

    full = lambda shape: pl.BlockSpec(shape, lambda s: (0,) * len(shape))
    last = n_tiles - 1
    smem = pl.BlockSpec(memory_space=pltpu.SMEM)
    h1, route, xs, tab, wexp, wblk, nwork = pl.pallas_call(
        functools.partial(_mixer_kernel, tm=MIX_TILE, d_conv=d_conv, cap_chunks=cap // CHUNK,
                          tiles_per_seq=seq // MIX_TILE, max_work=max_work),
        grid=(n_tiles + 1,),
        in_specs=[pl.BlockSpec((MIX_TILE, d), lambda s: (jnp.minimum(s, last), 0)),
                  full((N_META, d)), full((1, d)), full((d, 4 * d_conv)),
                  full((CONV_WIDTH, d_conv)), full((1, d_conv)),
                  full((len(POOL_WINDOWS), LANES, LANES)), full((1, d_conv)), full((1, d_conv)),
                  full((2 * d_conv, d)), full((1, d)), full((ROUTER_ROWS, d)),
                  full((MIX_TILE, MIX_TILE))],
        out_specs=[pl.BlockSpec((MIX_TILE, d), lambda s: (s, 0)),
                   pl.BlockSpec((MIX_TILE, LANES), lambda s: (jnp.maximum(s - 1, 0), 0)),
                   pl.BlockSpec(memory_space=pl.ANY), smem, smem, smem, smem],
        out_shape=[jax.ShapeDtypeStruct((t + MIX_TILE, d), jnp.float32),
                   jax.ShapeDtypeStruct((t, LANES), jnp.float32),
                   jax.ShapeDtypeStruct((n_chunks, CHUNK, d), bf),
                   jax.ShapeDtypeStruct((n_tiles * TAB_STRIDE,), i32),
                   jax.ShapeDtypeStruct((max_work,), i32),
                   jax.ShapeDtypeStruct((max_work,), i32),
                   jax.ShapeDtypeStruct((1,), i32)],
        scratch_shapes=[pltpu.VMEM((MIX_TILE + HALO, d_conv), jnp.float32),
                        pltpu.VMEM((MIX_TILE + HALO, d_conv), jnp.float32),
                        pltpu.VMEM((2, MIX_TILE, d), bf),
                        pltpu.VMEM((3, STAGE_CHUNKS + SEGMENT_CHUNKS, CHUNK, d), bf),
                        pltpu.VMEM((WAIT_GROUP, CHUNK, d), bf),
                        pltpu.VMEM((SUBLANES, LANES), i32),
                        pltpu.SMEM((2, SUBLANES, LANES), i32),
                        pltpu.SMEM((N_EXPERTS,), i32),
                        pltpu.SMEM((1,), i32),
                        pltpu.SemaphoreType.DMA,
                        pltpu.SemaphoreType.DMA],
        compiler_params=pltpu.CompilerParams(
            dimension_semantics=("arbitrary",), vmem_limit_bytes=VMEM_LIMIT),
        name="mixer",
    )(x.reshape(t, d), meta_tokens, row(norm_mix), w_in[0].astype(bf), conv_w[0],
      row(norm_conv_out), pool_w[0].astype(bf), row(pool_scale), row(norm_pool_out),
      w_out[0].astype(bf), row(norm_ffn), wrt, tri)

    ys = pl.pallas_call(
        functools.partial(_moe_kernel, d_exp=d_exp),
        grid_spec=pltpu.PrefetchScalarGridSpec(
            num_scalar_prefetch=3,
            grid=(max_work,),
            in_specs=[pl.BlockSpec((MOE_TILE, d), lambda w, ex, bk, n: (bk[w], 0)),
                      pl.BlockSpec((1, d, d_exp), lambda w, ex, bk, n: (ex[w], 0, 0)),
                      pl.BlockSpec((1, d, d_exp), lambda w, ex, bk, n: (ex[w], 0, 0)),
                      pl.BlockSpec((1, d_exp, d), lambda w, ex, bk, n: (ex[w], 0, 0))],
            out_specs=pl.BlockSpec((MOE_TILE, d), lambda w, ex, bk, n: (bk[w], 0)),
            scratch_shapes=[pltpu.VMEM((d, 2 * d_exp), bf), pltpu.VMEM((d_exp, d), bf)]),
        out_shape=jax.ShapeDtypeStruct((n_chunks * CHUNK, d), bf),
        compiler_params=pltpu.CompilerParams(
            dimension_semantics=("arbitrary",), vmem_limit_bytes=VMEM_LIMIT),
        name="moe",
    )(wexp, wblk, nwork, xs.reshape(n_chunks * CHUNK, d), w_gate[0], w_up[0], w_down[0])

    out = pl.pallas_call(
        functools.partial(_combine_kernel, tm=MIX_TILE, cap_chunks=cap // CHUNK),
        grid_spec=pltpu.PrefetchScalarGridSpec(
            num_scalar_prefetch=1,
            grid=(n_tiles,),
            in_specs=[pl.BlockSpec((MIX_TILE, d), lambda s, tb: (s, 0)),
                      pl.BlockSpec((MIX_TILE, LANES), lambda s, tb: (s, 0)),
                      pl.BlockSpec((1, d), lambda s, tb: (0, 0)),
                      pl.BlockSpec(memory_space=pl.ANY)],
            out_specs=pl.BlockSpec((MIX_TILE, d), lambda s, tb: (s, 0)),
            scratch_shapes=[pltpu.VMEM((3, STAGE_CHUNKS, CHUNK, d), bf),
                            pltpu.SemaphoreType.DMA((3,))]),
        out_shape=jax.ShapeDtypeStruct((t, d), jnp.float32),
        compiler_params=pltpu.CompilerParams(
            dimension_semantics=("arbitrary",), vmem_limit_bytes=VMEM_LIMIT),
        name="combine",
    )(tab, h1, route, row(final_norm), ys.reshape(n_chunks, CHUNK, d))
    return out.reshape(bsz, seq, d)
```

```python
import functools

import jax
import jax.numpy as jnp
from jax import lax
from jax.experimental import pallas as pl
from jax.experimental.pallas import tpu as pltpu

EPS = 1e-6
N_META = 16
CONV_WIDTH = 3
POOL_WINDOWS = (2, 4, 8, 16)
N_EXPERT_GROUPS = 4
EXPERTS_PER_GROUP = 8
N_EXPERTS = N_EXPERT_GROUPS * EXPERTS_PER_GROUP
LANES = 128
SUBLANES = 8
HALO = 16
CHUNK = 16
ROUTER_ROWS = 64

MIX_TILE = 512
MOE_TILE = 1024
MOE_CHUNKS = MOE_TILE // CHUNK
MOE_CHUNKS_LOG2 = MOE_CHUNKS.bit_length() - 1
VMEM_LIMIT = 56 * 1024 * 1024

STAGE_ROWS = 2 * MIX_TILE + N_EXPERTS * CHUNK
STAGE_CHUNKS = STAGE_ROWS // CHUNK
PERM_ROWS = 512
TAB_STRIDE = 2 * N_EXPERTS + 1
UNROLLED_CHUNKS = 5
SEGMENT_CHUNKS = 5
WAIT_GROUP = 16
WAIT_GROUP_LOG2 = WAIT_GROUP.bit_length() - 1


def _rms(x, g):
    return x * lax.rsqrt(jnp.mean(x * x, axis=-1, keepdims=True) + EPS) * g


def _route(lt):
    tm = lt.shape[1]
    sub = lax.broadcasted_iota(jnp.int32, (SUBLANES, tm), 0)
    neg = jnp.float32(-jnp.inf)
    big = jnp.int32(1 << 20)
    lg = jnp.where(sub < N_EXPERT_GROUPS, lt[N_EXPERTS:N_EXPERTS + SUBLANES], neg)
    gmax = jnp.max(lg, axis=0, keepdims=True)
    g1 = 1.0 / jnp.sum(jnp.exp(lg - gmax), axis=0, keepdims=True)
    gsel = jnp.min(jnp.where(lg == gmax, sub, big), axis=0, keepdims=True)
    le = lt[(N_EXPERT_GROUPS - 1) * SUBLANES:N_EXPERT_GROUPS * SUBLANES]
    for g in range(N_EXPERT_GROUPS - 2, -1, -1):
        le = jnp.where(gsel == g, lt[g * SUBLANES:(g + 1) * SUBLANES], le)
    m1 = jnp.max(le, axis=0, keepdims=True)
    pe = jnp.exp(le - m1)
    i1 = jnp.min(jnp.where(le == m1, sub, big), axis=0, keepdims=True)
    pe2 = jnp.where(sub == i1, -1.0, pe)
    p2 = jnp.max(pe2, axis=0, keepdims=True)
    i2 = jnp.min(jnp.where(pe2 == p2, sub, big), axis=0, keepdims=True)
    inv = 1.0 / (1.0 + p2)
    return (gsel * EXPERTS_PER_GROUP + i1, gsel * EXPERTS_PER_GROUP + i2,
            g1 * inv, g1 * p2 * inv)


def _derived_zero(x):
    bits = pltpu.bitcast(x, jnp.int32)
    return lax.shift_right_logical(lax.shift_right_logical(bits, 16), 16).astype(jnp.float32)


def _copy_chunks(src_ref, src_chunk, dst_ref, dst_chunk, sem, n=1):
    return pltpu.make_async_copy(src_ref.at[pl.ds(src_chunk, n)],
                                 dst_ref.at[pl.ds(dst_chunk, n)], sem)


def _drain(n_chunks, src_ref, dst_ref, sem):
    def group_body(k, c):
        _copy_chunks(src_ref, 0, dst_ref, 0, sem, WAIT_GROUP).wait()
        return c
    lax.fori_loop(0, lax.shift_right_logical(n_chunks, WAIT_GROUP_LOG2), group_body, 0)

    def one_body(k, c):
        _copy_chunks(src_ref, 0, dst_ref, 0, sem).wait()
        return c
    lax.fori_loop(0, n_chunks & (WAIT_GROUP - 1), one_body, 0)


def _issue_head(nch, src_ref, src0, dst_ref, dst0, sem):
    for k in range(UNROLLED_CHUNKS):
        @pl.when(k < nch)
        def _():
            _copy_chunks(src_ref, src0 + k, dst_ref, dst0 + k, sem).start()


def _issue_rest(first, nch, src_ref, src0, dst_ref, dst0, sem):
    def body(k, c):
        _copy_chunks(src_ref, src0 + k, dst_ref, dst0 + k, sem).start()
        return c
    lax.fori_loop(first, nch, body, 0)


def _mixer_kernel(x_ref, meta_ref, nmix_ref, win_ref, convw_ref, nconv_ref, poolw_ref,
                  pscale_ref, npool_ref, wout_ref, nffn_ref, wrt_ref, tri_ref,
                  h1_ref, route_ref, xs_ref, tab_ref, wexp_ref, wblk_ref, nwork_ref,
                  extu_ref, extp_ref, hf_ref, perm_ref, stage_ref, zero_ref, info_v_ref, info_s_ref,
                  run_ref, pend_ref, sem_info, sem_out,
                  *, tm, d_conv, cap_chunks, tiles_per_seq, max_work):
    s = pl.program_id(0)
    n_tiles = pl.num_programs(0) - 1
    cur = s % 2
    prv = 1 - cur
    st_sort = (s + 2) % 3
    st_disp = (s + 1) % 3
    nmix = nmix_ref[...]

    def info_copy(slot):
        return pltpu.make_async_copy(info_v_ref, info_s_ref.at[slot], sem_info)

    @pl.when(s == 0)
    def _():
        pend_ref[0] = 0
        zero_ref[...] = jnp.zeros_like(zero_ref)
        stage_ref[...] = jnp.zeros_like(stage_ref)
        hf_ref[...] = jnp.zeros_like(hf_ref)
        info_v_ref[...] = jnp.zeros_like(info_v_ref)
        info_copy(0).start()
        info_copy(0).wait()
        info_copy(1).start()

        def init(e, c):
            run_ref[e] = 0
            return c
        lax.fori_loop(0, N_EXPERTS, init, 0)

    @pl.when(s % tiles_per_seq == 0)
    def _():
        hm = _rms(meta_ref[...], nmix).astype(jnp.bfloat16)
        pm = jnp.dot(hm, win_ref[:, d_conv:], preferred_element_type=jnp.float32)
        extu_ref[0:HALO, :] = pm[:, 0:d_conv] * pm[:, d_conv:2 * d_conv]
        extp_ref[0:HALO, :] = pm[:, 2 * d_conv:]

    info_copy(cur).wait()

    hfp = hf_ref[prv]
    lt = lax.dot_general(wrt_ref[...], hfp, (((1,), (1,)), ((), ())),
                         preferred_element_type=jnp.float32)
    e0, e1, gate0, gate1 = _route(lt)
    esub = lax.broadcasted_iota(jnp.int32, (N_EXPERTS, tm), 0)
    dead = (s == 0).astype(jnp.int32)
    oh0 = esub == e0 - dead * N_EXPERTS
    oh1 = esub == e1 - dead * N_EXPERTS
    oh = jnp.where(oh0, 1.0, jnp.where(oh1, 1.0, 0.0))
    ct = jnp.dot(oh.astype(jnp.bfloat16), tri_ref[...], preferred_element_type=jnp.float32)
    n_col = jnp.sum(oh, axis=1, keepdims=True)
    nch_col = jnp.floor((n_col + (CHUNK - 1)) * (1.0 / CHUNK))
    e_sub = lax.broadcasted_iota(jnp.int32, (N_EXPERTS, LANES), 0)
    l_lane = lax.broadcasted_iota(jnp.int32, (N_EXPERTS, LANES), 1)
    nch_row = jnp.sum(jnp.where(e_sub == l_lane, nch_col, 0.0), axis=0, keepdims=True)
    coff_col = jnp.sum(jnp.where(l_lane < e_sub, nch_row, 0.0), axis=1, keepdims=True)
    coff_row = jnp.sum(jnp.where(e_sub == l_lane, coff_col, 0.0), axis=0, keepdims=True)
    info_v_ref[...] = jnp.concatenate(
        [nch_row, coff_row, jnp.zeros((SUBLANES - 2, LANES), jnp.float32)], axis=0
    ).astype(jnp.int32)
    info_copy(prv).start()

    posbase = ct + coff_col * CHUNK
    pos0 = jnp.sum(jnp.where(oh0, posbase, 0.0), axis=0, keepdims=True)
    pos1 = jnp.sum(jnp.where(oh1, posbase, 0.0), axis=0, keepdims=True)
    route_t = jnp.concatenate(
        [pos0, pos1, gate0, gate1, jnp.zeros((LANES - 4, tm), jnp.float32)], axis=0)
    route_ref[...] = route_t.T

    pos0i = pos0.astype(jnp.int32)
    pos1i = pos1.astype(jnp.int32)
    for r0 in range(0, STAGE_ROWS, PERM_ROWS):
        r = lax.broadcasted_iota(jnp.int32, (PERM_ROWS, tm), 0) + r0
        perm_ref[r0:r0 + PERM_ROWS, :] = jnp.where(
            r == pos0i, 1.0, jnp.where(r == pos1i, 1.0, 0.0)).astype(jnp.bfloat16)

    x = x_ref[...]
    hn = _rms(x, nmix).astype(jnp.bfloat16)
    proj = jnp.dot(hn, win_ref[...], preferred_element_type=jnp.float32)
    b_gate = proj[:, 0:d_conv]
    v_pool = proj[:, 3 * d_conv:]
    extu_ref[HALO:, :] = proj[:, d_conv:2 * d_conv] * proj[:, 2 * d_conv:3 * d_conv]
    extp_ref[HALO:, :] = v_pool

    anchor = jnp.concatenate([proj[tm - 1:tm, 0:d_conv], proj[tm - 1:tm, 3 * d_conv:]], axis=1)
    hf_late = hfp + _derived_zero(anchor).astype(jnp.bfloat16)
    for r0 in range(0, STAGE_ROWS, PERM_ROWS):
        rows = jnp.dot(perm_ref[r0:r0 + PERM_ROWS, :], hf_late, preferred_element_type=jnp.float32)
        stage_ref[st_sort, r0 // CHUNK:(r0 + PERM_ROWS) // CHUNK] = (
            rows.astype(jnp.bfloat16).reshape(PERM_ROWS // CHUNK, CHUNK, x_ref.shape[-1]))

    conv = convw_ref[CONV_WIDTH - 1:CONV_WIDTH, :] * extu_ref[HALO:, :]
    for k in range(CONV_WIDTH - 1):
        off = HALO - (CONV_WIDTH - 1 - k)
        conv = conv + convw_ref[k:k + 1, :] * extu_ref[off:off + tm, :]
    y_conv = _rms(b_gate * conv, nconv_ref[...])

    mixed = []
    for g, w in enumerate(POOL_WINDOWS):
        cols = slice(g * LANES, (g + 1) * LANES)
        acc = extp_ref[:, cols]
        span = 1
        while span < w:
            acc = acc + pltpu.roll(acc, span, 0)
            span *= 2
        pooled = acc[HALO:] * (1.0 / w) - v_pool[:, cols]
        mixed.append(jnp.dot(pooled.astype(jnp.bfloat16), poolw_ref[g],
                             preferred_element_type=jnp.float32))
    y_pool = _rms(jnp.concatenate(mixed, axis=-1) * pscale_ref[...], npool_ref[...])

    y = jnp.concatenate([y_conv, y_pool], axis=-1).astype(jnp.bfloat16)
    h1 = x + jnp.dot(y, wout_ref[...], preferred_element_type=jnp.float32)
    h1_ref[...] = h1
    hf_ref[cur] = _rms(h1, nffn_ref[...]).astype(jnp.bfloat16)

    extu_ref[0:HALO, :] = extu_ref[tm:tm + HALO, :]
    extp_ref[0:HALO, :] = extp_ref[tm:tm + HALO, :]

    pending = pend_ref[0]
    tile2 = jnp.maximum(s - 2, 0)
    tot = 0
    issued = 0
    most = 0
    plan = []
    for e in range(N_EXPERTS):
        nch = info_s_ref[cur, 0, e]
        base = run_ref[e]
        tab_ref[tile2 * TAB_STRIDE + e] = base
        tab_ref[tile2 * TAB_STRIDE + N_EXPERTS + e] = nch
        run_ref[e] = base + nch
        plan.append((nch > 0, info_s_ref[cur, 1, e], e * cap_chunks + base))
        tot = tot + nch
        issued = issued + jnp.where(nch > 0, SEGMENT_CHUNKS, 0) + jnp.maximum(nch - SEGMENT_CHUNKS, 0)
        most = jnp.maximum(most, nch)
    tab_ref[tile2 * TAB_STRIDE + 2 * N_EXPERTS] = tot
    pend_ref[0] = issued
    _drain(pending, zero_ref, xs_ref, sem_out)
    for nonempty, src0, dst0 in plan:
        @pl.when(nonempty)
        def _():
            _copy_chunks(stage_ref.at[st_disp], src0, xs_ref, dst0, sem_out, SEGMENT_CHUNKS).start()

    @pl.when(most > SEGMENT_CHUNKS)
    def _():
        def body(e, c):
            _issue_rest(SEGMENT_CHUNKS, info_s_ref[cur, 0, e],
                        stage_ref.at[st_disp], info_s_ref[cur, 1, e],
                        xs_ref, e * cap_chunks + tab_ref[tile2 * TAB_STRIDE + e], sem_out)
            return c
        lax.fori_loop(0, N_EXPERTS, body, 0)

    @pl.when(s == n_tiles)
    def _():
        info_copy(prv).wait()
        _drain(pend_ref[0], zero_ref, xs_ref, sem_out)
        sem_last = sem_out

        def last_body(e, tot):
            nch = info_s_ref[prv, 0, e]
            base = run_ref[e]
            tab_ref[(s - 1) * TAB_STRIDE + e] = base
            tab_ref[(s - 1) * TAB_STRIDE + N_EXPERTS + e] = nch
            _issue_rest(0, nch, stage_ref.at[st_sort], info_s_ref[prv, 1, e],
                        xs_ref, e * cap_chunks + base, sem_last)
            run_ref[e] = base + nch
            return tot + nch
        last = lax.fori_loop(0, N_EXPERTS, last_body, 0)
        tab_ref[(s - 1) * TAB_STRIDE + 2 * N_EXPERTS] = last
        _drain(last, zero_ref, xs_ref, sem_last)

        def pad_body(e, carry):
            tot, w = carry
            n = run_ref[e]
            tiles = lax.shift_right_logical(n + (MOE_CHUNKS - 1), MOE_CHUNKS_LOG2)
            nch = tiles * MOE_CHUNKS - n

            def chunk_body(k, c):
                _copy_chunks(zero_ref, 0, xs_ref, e * cap_chunks + n + k, sem_last).start()
                return c
            lax.fori_loop(0, nch, chunk_body, 0)

            def work_body(j, c):
                wexp_ref[w + j] = e
                wblk_ref[w + j] = e * (cap_chunks // MOE_CHUNKS) + j
                return c
            lax.fori_loop(0, tiles, work_body, 0)
            return tot + nch, w + tiles
        n_pad, n_work = lax.fori_loop(0, N_EXPERTS, pad_body, (0, 0))
        _drain(n_pad, zero_ref, xs_ref, sem_last)
        nwork_ref[0] = n_work

        def fill_body(w, c):
            wexp_ref[w] = wexp_ref[n_work - 1]
            wblk_ref[w] = wblk_ref[n_work - 1]
            return c
        lax.fori_loop(n_work, max_work, fill_body, 0)


def _moe_kernel(exp_ref, blk_ref, n_ref, xs_ref, wg_ref, wu_ref, wd_ref, ys_ref,
                wgu_bf_ref, wd_bf_ref, *, d_exp):
    w = pl.program_id(0)

    @pl.when((w == 0) | (exp_ref[w] != exp_ref[jnp.maximum(w - 1, 0)]))
    def _():
        wgu_bf_ref[:, :d_exp] = wg_ref[0].astype(jnp.bfloat16)
        wgu_bf_ref[:, d_exp:] = wu_ref[0].astype(jnp.bfloat16)
        wd_bf_ref[...] = wd_ref[0].astype(jnp.bfloat16)

    @pl.when(w < n_ref[0])
    def _():
        gu = jnp.dot(xs_ref[...], wgu_bf_ref[...], preferred_element_type=jnp.float32)
        g = gu[:, :d_exp]
        a = g * jax.nn.sigmoid(g) * gu[:, d_exp:]
        ys_ref[...] = jnp.dot(a.astype(jnp.bfloat16), wd_bf_ref[...],
                              preferred_element_type=jnp.float32).astype(jnp.bfloat16)


def _combine_kernel(tab_ref, h1_ref, route_ref, fn_ref, ys_ref, out_ref, stage_ref, sem,
                    *, tm, cap_chunks):
    s = pl.program_id(0)
    n_steps = pl.num_programs(0)
    slot = s % 3
    ahead = (s + 2) % 3

    @pl.when(s == 0)
    def _():
        stage_ref[...] = jnp.zeros_like(stage_ref)
        for j in range(2):
            def body(e, coff):
                nch = tab_ref[j * TAB_STRIDE + N_EXPERTS + e]
                _issue_rest(0, nch, ys_ref, e * cap_chunks + tab_ref[j * TAB_STRIDE + e],
                            stage_ref.at[j], coff, sem.at[j])
                return coff + nch
            lax.fori_loop(0, N_EXPERTS, body, 0)

    _drain(tab_ref[s * TAB_STRIDE + 2 * N_EXPERTS], ys_ref, stage_ref.at[slot], sem.at[slot])

    route = route_ref[...]
    pos0 = route[:, 0:1].astype(jnp.int32)
    pos1 = route[:, 1:2].astype(jnp.int32)
    gate0 = route[:, 2:3]
    gate1 = route[:, 3:4]
    r = lax.broadcasted_iota(jnp.int32, (tm, STAGE_ROWS), 1)
    pw = jnp.where(r == pos0, gate0, jnp.where(r == pos1, gate1, 0.0)).astype(jnp.bfloat16)
    moe = jnp.dot(pw, stage_ref[slot].reshape(STAGE_ROWS, out_ref.shape[-1]),
                  preferred_element_type=jnp.float32)
    out_ref[...] = _rms(h1_ref[...] + moe, fn_ref[...])

    nxt = jnp.minimum(s + 2, n_steps - 1)
    has_next = (s + 2 < n_steps).astype(jnp.int32)
    coff = 0
    most = 0
    for e in range(N_EXPERTS):
        nch = tab_ref[nxt * TAB_STRIDE + N_EXPERTS + e] * has_next
        _issue_head(nch, ys_ref, e * cap_chunks + tab_ref[nxt * TAB_STRIDE + e],
                    stage_ref.at[ahead], coff, sem.at[ahead])
        coff = coff + nch
        most = jnp.maximum(most, nch)

    @pl.when(most > UNROLLED_CHUNKS)
    def _():
        def body(e, coff):
            nch = tab_ref[nxt * TAB_STRIDE + N_EXPERTS + e]
            _issue_rest(UNROLLED_CHUNKS, nch, ys_ref, e * cap_chunks + tab_ref[nxt * TAB_STRIDE + e],
                        stage_ref.at[ahead], coff, sem.at[ahead])
            return coff + nch
        lax.fori_loop(0, N_EXPERTS, body, 0)


def kernel(x, meta_tokens, norm_mix, w_in, conv_w, norm_conv_out, pool_w, pool_scale,
           norm_pool_out, w_out, norm_ffn, w_router_group, w_router_expert,
           w_gate, w_up, w_down, final_norm):
    bsz, seq, d = x.shape
    d_conv = conv_w.shape[-1]
    d_exp = w_gate.shape[-1]
    n_exp = w_gate.shape[1]
    assert meta_tokens.shape[0] == N_META == HALO and n_exp == N_EXPERTS
    assert seq % MIX_TILE == 0
    bf = jnp.bfloat16
    i32 = jnp.int32
    t = bsz * seq
    n_tiles = t // MIX_TILE
    cap = -(-(t + CHUNK * n_tiles) // MOE_TILE) * MOE_TILE + MOE_TILE
    n_chunks = N_EXPERTS * cap // CHUNK
    max_work = (2 * t + N_EXPERTS * (CHUNK - 1) * n_tiles) // MOE_TILE + N_EXPERTS
    row = lambda a: a.reshape(1, -1)
    wrt = jnp.concatenate(
        [w_router_expert[0].T, w_router_group[0].T,
         jnp.zeros((ROUTER_ROWS - N_EXPERTS - N_EXPERT_GROUPS, d), jnp.float32)], axis=0).astype(bf)
    tri = (lax.broadcasted_iota(i32, (MIX_TILE, MIX_TILE), 0)
           < lax.broadcasted_iota(i32, (MIX_TILE, MIX_TILE), 1)).astype(bf)

    full = lambda shape: pl.BlockSpec(shape, lambda s: (0,) * len(shape))
    last = n_tiles - 1
    smem = pl.BlockSpec(memory_space=pltpu.SMEM)
    h1, route, xs, tab, wexp, wblk, nwork = pl.pallas_call(
        functools.partial(_mixer_kernel, tm=MIX_TILE, d_conv=d_conv, cap_chunks=cap // CHUNK,
                          tiles_per_seq=seq // MIX_TILE, max_work=max_work),
        grid=(n_tiles + 1,),
        in_specs=[pl.BlockSpec((MIX_TILE, d), lambda s: (jnp.minimum(s, last), 0)),
                  full((N_META, d)), full((1, d)), full((d, 4 * d_conv)),
                  full((CONV_WIDTH, d_conv)), full((1, d_conv)),
                  full((len(POOL_WINDOWS), LANES, LANES)), full((1, d_conv)), full((1, d_conv)),
                  full((2 * d_conv, d)), full((1, d)), full((ROUTER_ROWS, d)),
                  full((MIX_TILE, MIX_TILE))],
        out_specs=[pl.BlockSpec((MIX_TILE, d), lambda s: (s, 0)),
                   pl.BlockSpec((MIX_TILE, LANES), lambda s: (jnp.maximum(s - 1, 0), 0)),
                   pl.BlockSpec(memory_space=pl.ANY), smem, smem, smem, smem],
        out_shape=[jax.ShapeDtypeStruct((t + MIX_TILE, d), jnp.float32),
                   jax.ShapeDtypeStruct((t, LANES), jnp.float32),
                   jax.ShapeDtypeStruct((n_chunks, CHUNK, d), bf),
                   jax.ShapeDtypeStruct((n_tiles * TAB_STRIDE,), i32),
                   jax.ShapeDtypeStruct((max_work,), i32),
                   jax.ShapeDtypeStruct((max_work,), i32),
                   jax.ShapeDtypeStruct((1,), i32)],
        scratch_shapes=[pltpu.VMEM((MIX_TILE + HALO, d_conv), jnp.float32),
                        pltpu.VMEM((MIX_TILE + HALO, d_conv), jnp.float32),
                        pltpu.VMEM((2, MIX_TILE, d), bf),
                        pltpu.VMEM((STAGE_ROWS, MIX_TILE), bf),
                        pltpu.VMEM((3, STAGE_CHUNKS + SEGMENT_CHUNKS, CHUNK, d), bf),
                        pltpu.VMEM((WAIT_GROUP, CHUNK, d), bf),
                        pltpu.VMEM((SUBLANES, LANES), i32),
                        pltpu.SMEM((2, SUBLANES, LANES), i32),
                        pltpu.SMEM((N_EXPERTS,), i32),
                        pltpu.SMEM((1,), i32),
                        pltpu.SemaphoreType.DMA,
                        pltpu.SemaphoreType.DMA],
        compiler_params=pltpu.CompilerParams(
            dimension_semantics=("arbitrary",), vmem_limit_bytes=VMEM_LIMIT),
        name="mixer",
    )(x.reshape(t, d), meta_tokens, row(norm_mix), w_in[0].astype(bf), conv_w[0],
      row(norm_conv_out), pool_w[0].astype(bf), row(pool_scale), row(norm_pool_out),
      w_out[0].astype(bf), row(norm_ffn), wrt, tri)

    ys = pl.pallas_call(
        functools.partial(_moe_kernel, d_exp=d_exp),
        grid_spec=pltpu.PrefetchScalarGridSpec(
            num_scalar_prefetch=3,
            grid=(max_work,),
            in_specs=[pl.BlockSpec((MOE_TILE, d), lambda w, ex, bk, n: (bk[w], 0)),
                      pl.BlockSpec((1, d, d_exp), lambda w, ex, bk, n: (ex[w], 0, 0)),
                      pl.BlockSpec((1, d, d_exp), lambda w, ex, bk, n: (ex[w], 0, 0)),
                      pl.BlockSpec((1, d_exp, d), lambda w, ex, bk, n: (ex[w], 0, 0))],
            out_specs=pl.BlockSpec((MOE_TILE, d), lambda w, ex, bk, n: (bk[w], 0)),
            scratch_shapes=[pltpu.VMEM((d, 2 * d_exp), bf), pltpu.VMEM((d_exp, d), bf)]),
        out_shape=jax.ShapeDtypeStruct((n_chunks * CHUNK, d), bf),
        compiler_params=pltpu.CompilerParams(
            dimension_semantics=("arbitrary",), vmem_limit_bytes=VMEM_LIMIT),
        name="moe",
    )(wexp, wblk, nwork, xs.reshape(n_chunks * CHUNK, d), w_gate[0], w_up[0], w_down[0])

    out = pl.pallas_call(
        functools.partial(_combine_kernel, tm=MIX_TILE, cap_chunks=cap // CHUNK),
        grid_spec=pltpu.PrefetchScalarGridSpec(
            num_scalar_prefetch=1,
            grid=(n_tiles,),
            in_specs=[pl.BlockSpec((MIX_TILE, d), lambda s, tb: (s, 0)),
                      pl.BlockSpec((MIX_TILE, LANES), lambda s, tb: (s, 0)),
                      pl.BlockSpec((1, d), lambda s, tb: (0, 0)),
                      pl.BlockSpec(memory_space=pl.ANY)],
            out_specs=pl.BlockSpec((MIX_TILE, d), lambda s, tb: (s, 0)),
            scratch_shapes=[pltpu.VMEM((3, STAGE_CHUNKS, CHUNK, d), bf),
                            pltpu.SemaphoreType.DMA((3,))]),
        out_shape=jax.ShapeDtypeStruct((t, d), jnp.float32),
        compiler_params=pltpu.CompilerParams(
            dimension_semantics=("arbitrary",), vmem_limit_bytes=VMEM_LIMIT),
        name="combine",
    )(tab, h1, route, row(final_norm), ys.reshape(n_chunks, CHUNK, d))
    return out.reshape(bsz, seq, d)
```

```python
import functools

import jax
import jax.numpy as jnp
from jax import lax
from jax.experimental import pallas as pl
from jax.experimental.pallas import tpu as pltpu

EPS = 1e-6
N_META = 16
CONV_WIDTH = 3
POOL_WINDOWS = (2, 4, 8, 16)
N_EXPERT_GROUPS = 4
EXPERTS_PER_GROUP = 8
N_EXPERTS = N_EXPERT_GROUPS * EXPERTS_PER_GROUP
LANES = 128
SUBLANES = 8
HALO = 16
CHUNK = 16
ROUTER_ROWS = 64

MIX_TILE = 512
MOE_TILE = 1024
MOE_CHUNKS = MOE_TILE // CHUNK
MOE_CHUNKS_LOG2 = MOE_CHUNKS.bit_length() - 1
VMEM_LIMIT = 56 * 1024 * 1024

STAGE_ROWS = 2 * MIX_TILE + N_EXPERTS * CHUNK
STAGE_CHUNKS = STAGE_ROWS // CHUNK
PERM_ROWS = 512
SLOT_CHUNKS = STAGE_CHUNKS // N_EXPERTS
SLOT_ROWS = SLOT_CHUNKS * CHUNK
OVERFLOW_CHUNKS = 2 * MIX_TILE // CHUNK
OVERFLOW_BLK = 256
TAB_STRIDE = 2 * N_EXPERTS + 2
SEGMENT_CHUNKS = 5
WAIT_GROUP = 16
WAIT_GROUP_LOG2 = WAIT_GROUP.bit_length() - 1


def _rms(x, g):
    return x * lax.rsqrt(jnp.mean(x * x, axis=-1, keepdims=True) + EPS) * g


def _route(lt):
    tm = lt.shape[1]
    sub = lax.broadcasted_iota(jnp.int32, (SUBLANES, tm), 0)
    neg = jnp.float32(-jnp.inf)
    big = jnp.int32(1 << 20)
    lg = jnp.where(sub < N_EXPERT_GROUPS, lt[N_EXPERTS:N_EXPERTS + SUBLANES], neg)
    gmax = jnp.max(lg, axis=0, keepdims=True)
    g1 = 1.0 / jnp.sum(jnp.exp(lg - gmax), axis=0, keepdims=True)
    gsel = jnp.min(jnp.where(lg == gmax, sub, big), axis=0, keepdims=True)
    le = lt[(N_EXPERT_GROUPS - 1) * SUBLANES:N_EXPERT_GROUPS * SUBLANES]
    for g in range(N_EXPERT_GROUPS - 2, -1, -1):
        le = jnp.where(gsel == g, lt[g * SUBLANES:(g + 1) * SUBLANES], le)
    m1 = jnp.max(le, axis=0, keepdims=True)
    pe = jnp.exp(le - m1)
    i1 = jnp.min(jnp.where(le == m1, sub, big), axis=0, keepdims=True)
    pe2 = jnp.where(sub == i1, -1.0, pe)
    p2 = jnp.max(pe2, axis=0, keepdims=True)
    i2 = jnp.min(jnp.where(pe2 == p2, sub, big), axis=0, keepdims=True)
    inv = 1.0 / (1.0 + p2)
    return (gsel * EXPERTS_PER_GROUP + i1, gsel * EXPERTS_PER_GROUP + i2,
            g1 * inv, g1 * p2 * inv)


def _derived_zero(x):
    bits = pltpu.bitcast(x, jnp.int32)
    return lax.shift_right_logical(lax.shift_right_logical(bits, 16), 16).astype(jnp.float32)


def _copy_chunks(src_ref, src_chunk, dst_ref, dst_chunk, sem, n=1):
    return pltpu.make_async_copy(src_ref.at[pl.ds(src_chunk, n)],
                                 dst_ref.at[pl.ds(dst_chunk, n)], sem)


def _drain(n_chunks, src_ref, dst_ref, sem):
    def group_body(k, c):
        _copy_chunks(src_ref, 0, dst_ref, 0, sem, WAIT_GROUP).wait()
        return c
    lax.fori_loop(0, lax.shift_right_logical(n_chunks, WAIT_GROUP_LOG2), group_body, 0)

    def one_body(k, c):
        _copy_chunks(src_ref, 0, dst_ref, 0, sem).wait()
        return c
    lax.fori_loop(0, n_chunks & (WAIT_GROUP - 1), one_body, 0)


def _issue_rest(first, nch, src_ref, src0, dst_ref, dst0, sem):
    def body(k, c):
        _copy_chunks(src_ref, src0 + k, dst_ref, dst0 + k, sem).start()
        return c
    lax.fori_loop(first, nch, body, 0)


def _mixer_kernel(x_ref, meta_ref, nmix_ref, win_ref, convw_ref, nconv_ref, poolw_ref,
                  pscale_ref, npool_ref, wout_ref, nffn_ref, wrt_ref, tri_ref,
                  h1_ref, route_ref, xs_ref, tab_ref, wexp_ref, wblk_ref, nwork_ref,
                  extu_ref, extp_ref, hf_ref, perm_ref, stage_ref, zero_ref, info_v_ref, info_s_ref,
                  run_ref, pend_ref, sem_info, sem_out,
                  *, tm, d_conv, cap_chunks, tiles_per_seq, max_work):
    s = pl.program_id(0)
    n_tiles = pl.num_programs(0) - 1
    cur = s % 2
    prv = 1 - cur
    st_sort = (s + 2) % 3
    st_disp = (s + 1) % 3
    nmix = nmix_ref[...]

    def info_copy(slot):
        return pltpu.make_async_copy(info_v_ref, info_s_ref.at[slot], sem_info)

    @pl.when(s == 0)
    def _():
        pend_ref[0] = 0
        zero_ref[...] = jnp.zeros_like(zero_ref)
        stage_ref[...] = jnp.zeros_like(stage_ref)
        hf_ref[...] = jnp.zeros_like(hf_ref)
        info_v_ref[...] = jnp.zeros_like(info_v_ref)
        info_copy(0).start()
        info_copy(0).wait()
        info_copy(1).start()

        def init(e, c):
            run_ref[e] = 0
            return c
        lax.fori_loop(0, N_EXPERTS, init, 0)

    @pl.when(s % tiles_per_seq == 0)
    def _():
        hm = _rms(meta_ref[...], nmix).astype(jnp.bfloat16)
        pm = jnp.dot(hm, win_ref[:, d_conv:], preferred_element_type=jnp.float32)
        extu_ref[0:HALO, :] = pm[:, 0:d_conv] * pm[:, d_conv:2 * d_conv]
        extp_ref[0:HALO, :] = pm[:, 2 * d_conv:]

    info_copy(cur).wait()

    hfp = hf_ref[prv]
    lt = lax.dot_general(wrt_ref[...], hfp, (((1,), (1,)), ((), ())),
                         preferred_element_type=jnp.float32)
    e0, e1, gate0, gate1 = _route(lt)
    esub = lax.broadcasted_iota(jnp.int32, (N_EXPERTS, tm), 0)
    dead = (s == 0).astype(jnp.int32)
    oh0 = esub == e0 - dead * N_EXPERTS
    oh1 = esub == e1 - dead * N_EXPERTS
    oh = jnp.where(oh0, 1.0, jnp.where(oh1, 1.0, 0.0))
    ct = jnp.dot(oh.astype(jnp.bfloat16), tri_ref[...], preferred_element_type=jnp.float32)
    n_col = jnp.sum(oh, axis=1, keepdims=True)
    nch_col = jnp.floor((n_col + (CHUNK - 1)) * (1.0 / CHUNK))
    e_sub = lax.broadcasted_iota(jnp.int32, (N_EXPERTS, LANES), 0)
    l_lane = lax.broadcasted_iota(jnp.int32, (N_EXPERTS, LANES), 1)
    nch_row = jnp.sum(jnp.where(e_sub == l_lane, nch_col, 0.0), axis=0, keepdims=True)
    coff_col = jnp.sum(jnp.where(l_lane < e_sub, nch_row, 0.0), axis=1, keepdims=True)
    coff_row = jnp.sum(jnp.where(e_sub == l_lane, coff_col, 0.0), axis=0, keepdims=True)
    info_v_ref[...] = jnp.concatenate(
        [nch_row, coff_row, jnp.zeros((SUBLANES - 2, LANES), jnp.float32)], axis=0
    ).astype(jnp.int32)
    info_copy(prv).start()

    posbase = ct + coff_col * CHUNK
    pos0 = jnp.sum(jnp.where(oh0, posbase, 0.0), axis=0, keepdims=True)
    pos1 = jnp.sum(jnp.where(oh1, posbase, 0.0), axis=0, keepdims=True)

    ov_col = jnp.maximum(nch_col - SLOT_CHUNKS, 0.0)
    ov_row = jnp.sum(jnp.where(e_sub == l_lane, ov_col, 0.0), axis=0, keepdims=True)
    ovoff_col = jnp.sum(jnp.where(l_lane < e_sub, ov_row, 0.0), axis=1, keepdims=True)
    e_col = lax.broadcasted_iota(jnp.int32, (N_EXPERTS, 1), 0).astype(jnp.float32)
    slotbase = jnp.where(ct < SLOT_ROWS, e_col * SLOT_ROWS + ct,
                         STAGE_ROWS - SLOT_ROWS + ovoff_col * CHUNK + ct)
    spos0 = jnp.sum(jnp.where(oh0, slotbase, 0.0), axis=0, keepdims=True)
    spos1 = jnp.sum(jnp.where(oh1, slotbase, 0.0), axis=0, keepdims=True)
    route_t = jnp.concatenate(
        [spos0, spos1, gate0, gate1, jnp.zeros((LANES - 4, tm), jnp.float32)], axis=0)
    route_ref[...] = route_t.T

    pos0i = pos0.astype(jnp.int32)
    pos1i = pos1.astype(jnp.int32)
    for r0 in range(0, STAGE_ROWS, PERM_ROWS):
        r = lax.broadcasted_iota(jnp.int32, (PERM_ROWS, tm), 0) + r0
        perm_ref[r0:r0 + PERM_ROWS, :] = jnp.where(
            r == pos0i, 1.0, jnp.where(r == pos1i, 1.0, 0.0)).astype(jnp.bfloat16)

    x = x_ref[...]
    hn = _rms(x, nmix).astype(jnp.bfloat16)
    proj = jnp.dot(hn, win_ref[...], preferred_element_type=jnp.float32)
    b_gate = proj[:, 0:d_conv]
    v_pool = proj[:, 3 * d_conv:]
    extu_ref[HALO:, :] = proj[:, d_conv:2 * d_conv] * proj[:, 2 * d_conv:3 * d_conv]
    extp_ref[HALO:, :] = v_pool

    anchor = jnp.concatenate([proj[tm - 1:tm, 0:d_conv], proj[tm - 1:tm, 3 * d_conv:]], axis=1)
    hf_late = hfp + _derived_zero(anchor).astype(jnp.bfloat16)
    for r0 in range(0, STAGE_ROWS, PERM_ROWS):
        rows = jnp.dot(perm_ref[r0:r0 + PERM_ROWS, :], hf_late, preferred_element_type=jnp.float32)
        stage_ref[st_sort, r0 // CHUNK:(r0 + PERM_ROWS) // CHUNK] = (
            rows.astype(jnp.bfloat16).reshape(PERM_ROWS // CHUNK, CHUNK, x_ref.shape[-1]))

    conv = convw_ref[CONV_WIDTH - 1:CONV_WIDTH, :] * extu_ref[HALO:, :]
    for k in range(CONV_WIDTH - 1):
        off = HALO - (CONV_WIDTH - 1 - k)
        conv = conv + convw_ref[k:k + 1, :] * extu_ref[off:off + tm, :]
    y_conv = _rms(b_gate * conv, nconv_ref[...])

    mixed = []
    for g, w in enumerate(POOL_WINDOWS):
        cols = slice(g * LANES, (g + 1) * LANES)
        acc = extp_ref[:, cols]
        span = 1
        while span < w:
            acc = acc + pltpu.roll(acc, span, 0)
            span *= 2
        pooled = acc[HALO:] * (1.0 / w) - v_pool[:, cols]
        mixed.append(jnp.dot(pooled.astype(jnp.bfloat16), poolw_ref[g],
                             preferred_element_type=jnp.float32))
    y_pool = _rms(jnp.concatenate(mixed, axis=-1) * pscale_ref[...], npool_ref[...])

    y = jnp.concatenate([y_conv, y_pool], axis=-1).astype(jnp.bfloat16)
    h1 = x + jnp.dot(y, wout_ref[...], preferred_element_type=jnp.float32)
    h1_ref[...] = h1
    hf_ref[cur] = _rms(h1, nffn_ref[...]).astype(jnp.bfloat16)

    extu_ref[0:HALO, :] = extu_ref[tm:tm + HALO, :]
    extp_ref[0:HALO, :] = extp_ref[tm:tm + HALO, :]

    pending = pend_ref[0]
    tile2 = jnp.maximum(s - 2, 0)
    issued = 0
    fetched = 0
    overflow = 0
    most = 0
    plan = []
    for e in range(N_EXPERTS):
        nch = info_s_ref[cur, 0, e]
        base = run_ref[e]
        tab_ref[tile2 * TAB_STRIDE + e] = base
        tab_ref[tile2 * TAB_STRIDE + N_EXPERTS + e] = nch
        run_ref[e] = base + nch
        plan.append((nch > 0, info_s_ref[cur, 1, e], e * cap_chunks + base))
        issued = issued + jnp.where(nch > 0, SEGMENT_CHUNKS, 0) + jnp.maximum(nch - SEGMENT_CHUNKS, 0)
        fetched = fetched + jnp.where(nch > 0, SLOT_CHUNKS, 0)
        overflow = overflow + jnp.maximum(nch - SLOT_CHUNKS, 0)
        most = jnp.maximum(most, nch)
    tab_ref[tile2 * TAB_STRIDE + 2 * N_EXPERTS] = fetched + overflow
    tab_ref[tile2 * TAB_STRIDE + 2 * N_EXPERTS + 1] = overflow
    pend_ref[0] = issued
    _drain(pending, zero_ref, xs_ref, sem_out)
    for nonempty, src0, dst0 in plan:
        @pl.when(nonempty)
        def _():
            _copy_chunks(stage_ref.at[st_disp], src0, xs_ref, dst0, sem_out, SEGMENT_CHUNKS).start()

    @pl.when(most > SEGMENT_CHUNKS)
    def _():
        def body(e, c):
            _issue_rest(SEGMENT_CHUNKS, info_s_ref[cur, 0, e],
                        stage_ref.at[st_disp], info_s_ref[cur, 1, e],
                        xs_ref, e * cap_chunks + tab_ref[tile2 * TAB_STRIDE + e], sem_out)
            return c
        lax.fori_loop(0, N_EXPERTS, body, 0)

    @pl.when(s == n_tiles)
    def _():
        info_copy(prv).wait()
        _drain(pend_ref[0], zero_ref, xs_ref, sem_out)
        sem_last = sem_out

        def last_body(e, carry):
            tot, fetched, overflow = carry
            nch = info_s_ref[prv, 0, e]
            base = run_ref[e]
            tab_ref[(s - 1) * TAB_STRIDE + e] = base
            tab_ref[(s - 1) * TAB_STRIDE + N_EXPERTS + e] = nch
            _issue_rest(0, nch, stage_ref.at[st_sort], info_s_ref[prv, 1, e],
                        xs_ref, e * cap_chunks + base, sem_last)
            run_ref[e] = base + nch
            return (tot + nch, fetched + jnp.where(nch > 0, SLOT_CHUNKS, 0),
                    overflow + jnp.maximum(nch - SLOT_CHUNKS, 0))
        last, fetched, overflow = lax.fori_loop(0, N_EXPERTS, last_body, (0, 0, 0))
        tab_ref[(s - 1) * TAB_STRIDE + 2 * N_EXPERTS] = fetched + overflow
        tab_ref[(s - 1) * TAB_STRIDE + 2 * N_EXPERTS + 1] = overflow
        _drain(last, zero_ref, xs_ref, sem_last)

        def pad_body(e, carry):
            tot, w = carry
            n = run_ref[e]
            tiles = jnp.where(n > 0, lax.shift_right_logical(
                n + (SLOT_CHUNKS - 1) + (MOE_CHUNKS - 1), MOE_CHUNKS_LOG2), 0)
            nch = tiles * MOE_CHUNKS - jnp.where(n > 0, n, 0)

            def chunk_body(k, c):
                _copy_chunks(zero_ref, 0, xs_ref, e * cap_chunks + n + k, sem_last).start()
                return c
            lax.fori_loop(0, nch, chunk_body, 0)

            def work_body(j, c):
                wexp_ref[w + j] = e
                wblk_ref[w + j] = e * (cap_chunks // MOE_CHUNKS) + j
                return c
            lax.fori_loop(0, tiles, work_body, 0)
            return tot + nch, w + tiles
        n_pad, n_work = lax.fori_loop(0, N_EXPERTS, pad_body, (0, 0))
        _drain(n_pad, zero_ref, xs_ref, sem_last)
        nwork_ref[0] = n_work

        def fill_body(w, c):
            wexp_ref[w] = wexp_ref[n_work - 1]
            wblk_ref[w] = wblk_ref[n_work - 1]
            return c
        lax.fori_loop(n_work, max_work, fill_body, 0)


def _moe_kernel(exp_ref, blk_ref, n_ref, xs_ref, wg_ref, wu_ref, wd_ref, ys_ref,
                wgu_bf_ref, wd_bf_ref, *, d_exp):
    w = pl.program_id(0)

    @pl.when((w == 0) | (exp_ref[w] != exp_ref[jnp.maximum(w - 1, 0)]))
    def _():
        wgu_bf_ref[:, :d_exp] = wg_ref[0].astype(jnp.bfloat16)
        wgu_bf_ref[:, d_exp:] = wu_ref[0].astype(jnp.bfloat16)
        wd_bf_ref[...] = wd_ref[0].astype(jnp.bfloat16)

    @pl.when(w < n_ref[0])
    def _():
        gu = jnp.dot(xs_ref[...], wgu_bf_ref[...], preferred_element_type=jnp.float32)
        g = gu[:, :d_exp]
        a = g * jax.nn.sigmoid(g) * gu[:, d_exp:]
        ys_ref[...] = jnp.dot(a.astype(jnp.bfloat16), wd_bf_ref[...],
                              preferred_element_type=jnp.float32).astype(jnp.bfloat16)


def _combine_kernel(tab_ref, h1_ref, route_ref, fn_ref, ys_ref, out_ref, stage_ref, sem,
                    *, tm, cap_chunks):
    s = pl.program_id(0)
    n_steps = pl.num_programs(0)
    slot = s % 3
    ahead = (s + 2) % 3

    d = out_ref.shape[-1]

    def fetch_overflow(j, jslot):
        def body(e, ov):
            nch = tab_ref[j * TAB_STRIDE + N_EXPERTS + e]
            _issue_rest(SLOT_CHUNKS, nch, ys_ref, e * cap_chunks + tab_ref[j * TAB_STRIDE + e],
                        stage_ref.at[jslot], STAGE_CHUNKS - SLOT_CHUNKS + ov, sem.at[jslot])
            return ov + jnp.maximum(nch - SLOT_CHUNKS, 0)
        lax.fori_loop(0, N_EXPERTS, body, 0)

    def slot_copy(j, e, jslot):
        return _copy_chunks(ys_ref, e * cap_chunks + tab_ref[j * TAB_STRIDE + e],
                            stage_ref.at[jslot], e * SLOT_CHUNKS, sem.at[jslot], SLOT_CHUNKS)

    @pl.when(s == 0)
    def _():
        stage_ref[...] = jnp.zeros_like(stage_ref)
        for j in range(2):
            def body(e, c):
                @pl.when(tab_ref[j * TAB_STRIDE + N_EXPERTS + e] > 0)
                def _():
                    slot_copy(j, e, j).start()
                return c
            lax.fori_loop(0, N_EXPERTS, body, 0)
            fetch_overflow(j, j)

    _drain(tab_ref[s * TAB_STRIDE + 2 * N_EXPERTS], ys_ref, stage_ref.at[slot], sem.at[slot])

    route = route_ref[...]
    pos0 = route[:, 0:1].astype(jnp.int32)
    pos1 = route[:, 1:2].astype(jnp.int32)
    gate0 = route[:, 2:3]
    gate1 = route[:, 3:4]

    def weights(col0, n_cols):
        r = lax.broadcasted_iota(jnp.int32, (tm, n_cols), 1) + col0
        return jnp.where(r == pos0, gate0, jnp.where(r == pos1, gate1, 0.0)).astype(jnp.bfloat16)

    def slots_term():
        return jnp.dot(weights(0, STAGE_ROWS), stage_ref[slot, 0:STAGE_CHUNKS].reshape(STAGE_ROWS, d),
                       preferred_element_type=jnp.float32)

    n_over = tab_ref[s * TAB_STRIDE + 2 * N_EXPERTS + 1]

    @pl.when(n_over == 0)
    def _():
        out_ref[...] = _rms(h1_ref[...] + slots_term(), fn_ref[...])

    @pl.when(n_over > 0)
    def _():
        out_ref[...] = h1_ref[...] + slots_term()
        blk_chunks = OVERFLOW_BLK // CHUNK

        def body(j, c):
            rows = stage_ref[slot, pl.ds(STAGE_CHUNKS + j * blk_chunks, blk_chunks)]
            out_ref[...] += jnp.dot(weights(STAGE_ROWS + j * OVERFLOW_BLK, OVERFLOW_BLK),
                                    rows.reshape(OVERFLOW_BLK, d), preferred_element_type=jnp.float32)
            return c
        lax.fori_loop(0, (n_over + (blk_chunks - 1)) // blk_chunks, body, 0)
        out_ref[...] = _rms(out_ref[...], fn_ref[...])

    nxt = jnp.minimum(s + 2, n_steps - 1)
    has_next = s + 2 < n_steps
    plan = [(has_next & (tab_ref[nxt * TAB_STRIDE + N_EXPERTS + e] > 0),
             e * cap_chunks + tab_ref[nxt * TAB_STRIDE + e]) for e in range(N_EXPERTS)]
    for e, (nonempty, src0) in enumerate(plan):
        @pl.when(nonempty)
        def _():
            _copy_chunks(ys_ref, src0, stage_ref.at[ahead], e * SLOT_CHUNKS, sem.at[ahead],
                         SLOT_CHUNKS).start()

    @pl.when(has_next & (tab_ref[nxt * TAB_STRIDE + 2 * N_EXPERTS + 1] > 0))
    def _():
        fetch_overflow(nxt, ahead)


def kernel(x, meta_tokens, norm_mix, w_in, conv_w, norm_conv_out, pool_w, pool_scale,
           norm_pool_out, w_out, norm_ffn, w_router_group, w_router_expert,
           w_gate, w_up, w_down, final_norm):
    bsz, seq, d = x.shape
    d_conv = conv_w.shape[-1]
    d_exp = w_gate.shape[-1]
    n_exp = w_gate.shape[1]
    assert meta_tokens.shape[0] == N_META == HALO and n_exp == N_EXPERTS
    assert seq % MIX_TILE == 0
    bf = jnp.bfloat16
    i32 = jnp.int32
    t = bsz * seq
    n_tiles = t // MIX_TILE
    cap = -(-(t + CHUNK * n_tiles) // MOE_TILE) * MOE_TILE + MOE_TILE
    n_chunks = N_EXPERTS * cap // CHUNK
    max_work = (2 * t + N_EXPERTS * ((CHUNK - 1) * n_tiles + SLOT_ROWS)) // MOE_TILE + N_EXPERTS
    row = lambda a: a.reshape(1, -1)
    wrt = jnp.concatenate(
        [w_router_expert[0].T, w_router_group[0].T,
         jnp.zeros((ROUTER_ROWS - N_EXPERTS - N_EXPERT_GROUPS, d), jnp.float32)], axis=0).astype(bf)
    tri = (lax.broadcasted_iota(i32, (MIX_TILE, MIX_TILE), 0)
           < lax.broadcasted_iota(i32, (MIX_TILE, MIX_TILE), 1)).astype(bf)

    full = lambda shape: pl.BlockSpec(shape, lambda s: (0,) * len(shape))
    last = n_tiles - 1
    smem = pl.BlockSpec(memory_space=pltpu.SMEM)
    h1, route, xs, tab, wexp, wblk, nwork = pl.pallas_call(
        functools.partial(_mixer_kernel, tm=MIX_TILE, d_conv=d_conv, cap_chunks=cap // CHUNK,
                          tiles_per_seq=seq // MIX_TILE, max_work=max_work),
        grid=(n_tiles + 1,),
        in_specs=[pl.BlockSpec((MIX_TILE, d), lambda s: (jnp.minimum(s, last), 0)),
                  full((N_META, d)), full((1, d)), full((d, 4 * d_conv)),
                  full((CONV_WIDTH, d_conv)), full((1, d_conv)),
                  full((len(POOL_WINDOWS), LANES, LANES)), full((1, d_conv)), full((1, d_conv)),
                  full((2 * d_conv, d)), full((1, d)), full((ROUTER_ROWS, d)),
                  full((MIX_TILE, MIX_TILE))],
        out_specs=[pl.BlockSpec((MIX_TILE, d), lambda s: (s, 0)),
                   pl.BlockSpec((MIX_TILE, LANES), lambda s: (jnp.maximum(s - 1, 0), 0)),
                   pl.BlockSpec(memory_space=pl.ANY), smem, smem, smem, smem],
        out_shape=[jax.ShapeDtypeStruct((t + MIX_TILE, d), jnp.float32),
                   jax.ShapeDtypeStruct((t, LANES), jnp.float32),
                   jax.ShapeDtypeStruct((n_chunks, CHUNK, d), bf),
                   jax.ShapeDtypeStruct((n_tiles * TAB_STRIDE,), i32),
                   jax.ShapeDtypeStruct((max_work,), i32),
                   jax.ShapeDtypeStruct((max_work,), i32),
                   jax.ShapeDtypeStruct((1,), i32)],
        scratch_shapes=[pltpu.VMEM((MIX_TILE + HALO, d_conv), jnp.float32),
                        pltpu.VMEM((MIX_TILE + HALO, d_conv), jnp.float32),
                        pltpu.VMEM((2, MIX_TILE, d), bf),
                        pltpu.VMEM((STAGE_ROWS, MIX_TILE), bf),
                        pltpu.VMEM((3, STAGE_CHUNKS + SEGMENT_CHUNKS, CHUNK, d), bf),
                        pltpu.VMEM((WAIT_GROUP, CHUNK, d), bf),
                        pltpu.VMEM((SUBLANES, LANES), i32),
                        pltpu.SMEM((2, SUBLANES, LANES), i32),
                        pltpu.SMEM((N_EXPERTS,), i32),
                        pltpu.SMEM((1,), i32),
                        pltpu.SemaphoreType.DMA,
                        pltpu.SemaphoreType.DMA],
        compiler_params=pltpu.CompilerParams(
            dimension_semantics=("arbitrary",), vmem_limit_bytes=VMEM_LIMIT),
        name="mixer",
    )(x.reshape(t, d), meta_tokens, row(norm_mix), w_in[0].astype(bf), conv_w[0],
      row(norm_conv_out), pool_w[0].astype(bf), row(pool_scale), row(norm_pool_out),
      w_out[0].astype(bf), row(norm_ffn), wrt, tri)

    ys = pl.pallas_call(
        functools.partial(_moe_kernel, d_exp=d_exp),
        grid_spec=pltpu.PrefetchScalarGridSpec(
            num_scalar_prefetch=3,
            grid=(max_work,),
            in_specs=[pl.BlockSpec((MOE_TILE, d), lambda w, ex, bk, n: (bk[w], 0)),
                      pl.BlockSpec((1, d, d_exp), lambda w, ex, bk, n: (ex[w], 0, 0)),
                      pl.BlockSpec((1, d, d_exp), lambda w, ex, bk, n: (ex[w], 0, 0)),
                      pl.BlockSpec((1, d_exp, d), lambda w, ex, bk, n: (ex[w], 0, 0))],
            out_specs=pl.BlockSpec((MOE_TILE, d), lambda w, ex, bk, n: (bk[w], 0)),
            scratch_shapes=[pltpu.VMEM((d, 2 * d_exp), bf), pltpu.VMEM((d_exp, d), bf)]),
        out_shape=jax.ShapeDtypeStruct((n_chunks * CHUNK, d), bf),
        compiler_params=pltpu.CompilerParams(
            dimension_semantics=("arbitrary",), vmem_limit_bytes=VMEM_LIMIT),
        name="moe",
    )(wexp, wblk, nwork, xs.reshape(n_chunks * CHUNK, d), w_gate[0], w_up[0], w_down[0])

    out = pl.pallas_call(
        functools.partial(_combine_kernel, tm=MIX_TILE, cap_chunks=cap // CHUNK),
        grid_spec=pltpu.PrefetchScalarGridSpec(
            num_scalar_prefetch=1,
            grid=(n_tiles,),
            in_specs=[pl.BlockSpec((MIX_TILE, d), lambda s, tb: (s, 0)),
                      pl.BlockSpec((MIX_TILE, LANES), lambda s, tb: (s, 0)),
                      pl.BlockSpec((1, d), lambda s, tb: (0, 0)),
                      pl.BlockSpec(memory_space=pl.ANY)],
            out_specs=pl.BlockSpec((MIX_TILE, d), lambda s, tb: (s, 0)),
            scratch_shapes=[pltpu.VMEM((3, STAGE_CHUNKS + OVERFLOW_CHUNKS, CHUNK, d), bf),
                            pltpu.SemaphoreType.DMA((3,))]),
        out_shape=jax.ShapeDtypeStruct((t, d), jnp.float32),
        compiler_params=pltpu.CompilerParams(
            dimension_semantics=("arbitrary",), vmem_limit_bytes=VMEM_LIMIT),
        name="combine",
    )(tab, h1, route, row(final_norm), ys.reshape(n_chunks, CHUNK, d))
    return out.reshape(bsz, seq, d)
```

```python
import functools

import jax
import jax.numpy as jnp
from jax import lax
from jax.experimental import pallas as pl
from jax.experimental.pallas import tpu as pltpu

EPS = 1e-6
N_META = 16
CONV_WIDTH = 3
POOL_WINDOWS = (2, 4, 8, 16)
N_EXPERT_GROUPS = 4
EXPERTS_PER_GROUP = 8
N_EXPERTS = N_EXPERT_GROUPS * EXPERTS_PER_GROUP
LANES = 128
SUBLANES = 8
HALO = 16
CHUNK = 16

MIX_TILE = 512
MOE_TILE = 1024
MOE_CHUNKS = MOE_TILE // CHUNK
MOE_CHUNKS_LOG2 = MOE_CHUNKS.bit_length() - 1
VMEM_LIMIT = 56 * 1024 * 1024

STAGE_ROWS = 2 * MIX_TILE + N_EXPERTS * CHUNK
STAGE_CHUNKS = STAGE_ROWS // CHUNK
PERM_ROWS = 512
TAB_STRIDE = 2 * N_EXPERTS + 1
UNROLLED_CHUNKS = 5
SEGMENT_CHUNKS = 5
WAIT_GROUP = 16
WAIT_GROUP_LOG2 = WAIT_GROUP.bit_length() - 1


def _rms(x, g):
    return x * lax.rsqrt(jnp.mean(x * x, axis=-1, keepdims=True) + EPS) * g


def _route(lt):
    tm = lt.shape[1]
    sub = lax.broadcasted_iota(jnp.int32, (SUBLANES, tm), 0)
    neg = jnp.float32(-jnp.inf)
    big = jnp.int32(1 << 20)
    lg = jnp.where(sub < N_EXPERT_GROUPS, lt[N_EXPERTS:N_EXPERTS + SUBLANES], neg)
    gmax = jnp.max(lg, axis=0, keepdims=True)
    g1 = 1.0 / jnp.sum(jnp.exp(lg - gmax), axis=0, keepdims=True)
    gsel = jnp.min(jnp.where(lg == gmax, sub, big), axis=0, keepdims=True)
    le = lt[(N_EXPERT_GROUPS - 1) * SUBLANES:N_EXPERT_GROUPS * SUBLANES]
    for g in range(N_EXPERT_GROUPS - 2, -1, -1):
        le = jnp.where(gsel == g, lt[g * SUBLANES:(g + 1) * SUBLANES], le)
    m1 = jnp.max(le, axis=0, keepdims=True)
    pe = jnp.exp(le - m1)
    i1 = jnp.min(jnp.where(le == m1, sub, big), axis=0, keepdims=True)
    pe2 = jnp.where(sub == i1, -1.0, pe)
    p2 = jnp.max(pe2, axis=0, keepdims=True)
    i2 = jnp.min(jnp.where(pe2 == p2, sub, big), axis=0, keepdims=True)
    inv = 1.0 / (1.0 + p2)
    return (gsel * EXPERTS_PER_GROUP + i1, gsel * EXPERTS_PER_GROUP + i2,
            g1 * inv, g1 * p2 * inv)


def _derived_zero(x):
    bits = pltpu.bitcast(x, jnp.int32)
    return lax.shift_right_logical(lax.shift_right_logical(bits, 16), 16).astype(jnp.float32)


def _copy_chunks(src_ref, src_chunk, dst_ref, dst_chunk, sem, n=1):
    return pltpu.make_async_copy(src_ref.at[pl.ds(src_chunk, n)],
                                 dst_ref.at[pl.ds(dst_chunk, n)], sem)


def _drain(n_chunks, src_ref, dst_ref, sem):
    def group_body(k, c):
        _copy_chunks(src_ref, 0, dst_ref, 0, sem, WAIT_GROUP).wait()
        return c
    lax.fori_loop(0, lax.shift_right_logical(n_chunks, WAIT_GROUP_LOG2), group_body, 0)

    def one_body(k, c):
        _copy_chunks(src_ref, 0, dst_ref, 0, sem).wait()
        return c
    lax.fori_loop(0, n_chunks & (WAIT_GROUP - 1), one_body, 0)


def _issue_head(head, src_ref, src0, dst_ref, dst0, sem):
    pairs = lax.shift_right_logical(head, 1)
    for k in range(UNROLLED_CHUNKS // 2):
        @pl.when(k < pairs)
        def _():
            _copy_chunks(src_ref, src0 + 2 * k, dst_ref, dst0 + 2 * k, sem, 2).start()

    @pl.when((head & 1) == 1)
    def _():
        _copy_chunks(src_ref, src0 + 2 * pairs, dst_ref, dst0 + 2 * pairs, sem).start()


def _issue_rest(first, nch, src_ref, src0, dst_ref, dst0, sem):
    def body(k, c):
        _copy_chunks(src_ref, src0 + k, dst_ref, dst0 + k, sem).start()
        return c
    lax.fori_loop(first, nch, body, 0)


def _mixer_kernel(x_ref, xn_ref, meta_ref, nmix_ref, win_ref, convw_ref, nconv_ref, poolw_ref,
                  pscale_ref, npool_ref, wout_ref, nffn_ref, wr_ref,
                  h1_ref, route_ref, xs_ref, tab_ref, wexp_ref, wblk_ref, nwork_ref,
                  extu_ref, extp_ref, hn_ref, h1s_ref, perm_ref, stage_ref, zero_ref, info_v_ref,
                  info_s_ref, run_ref, pend_ref, sem_info, sem_out,
                  *, tm, d_conv, cap_chunks, tiles_per_seq, max_work):
    s = pl.program_id(0)
    n_tiles = pl.num_programs(0) - 1
    cur = s % 2
    prv = 1 - cur
    st_sort = (s + 2) % 3
    st_disp = (s + 1) % 3
    nmix = nmix_ref[...]

    def info_copy(slot):
        return pltpu.make_async_copy(info_v_ref, info_s_ref.at[slot], sem_info)

    @pl.when(s == 0)
    def _():
        pend_ref[0] = 0
        zero_ref[...] = jnp.zeros_like(zero_ref)
        stage_ref[...] = jnp.zeros_like(stage_ref)
        h1s_ref[...] = jnp.zeros_like(h1s_ref)
        hn_ref[1] = _rms(x_ref[...], nmix).astype(jnp.bfloat16)
        info_v_ref[...] = jnp.zeros_like(info_v_ref)
        info_copy(0).start()
        info_copy(0).wait()
        info_copy(1).start()

        def init(e, c):
            run_ref[e] = 0
            return c
        lax.fori_loop(0, N_EXPERTS, init, 0)

    @pl.when(s % tiles_per_seq == 0)
    def _():
        hm = _rms(meta_ref[...], nmix).astype(jnp.bfloat16)
        pm = jnp.dot(hm, win_ref[:, d_conv:], preferred_element_type=jnp.float32)
        extu_ref[0:HALO, :] = pm[:, 0:d_conv] * pm[:, d_conv:2 * d_conv]
        extp_ref[0:HALO, :] = pm[:, 2 * d_conv:]

    info_copy(cur).wait()

    hn_ref[0] = hn_ref[1]
    hn = hn_ref[0]
    proj_bc = jnp.dot(hn, win_ref[:, 0:2 * d_conv], preferred_element_type=jnp.float32)

    hfp = _rms(h1s_ref[...], nffn_ref[...]).astype(jnp.bfloat16)
    lt = jnp.dot(hfp, wr_ref[...], preferred_element_type=jnp.float32).T
    e0, e1, gate0, gate1 = _route(lt)
    esub = lax.broadcasted_iota(jnp.int32, (N_EXPERTS, tm), 0)
    dead = (s == 0).astype(jnp.int32)
    oh0 = esub == e0 - dead * N_EXPERTS
    oh1 = esub == e1 - dead * N_EXPERTS
    oh = jnp.where(oh0, 1.0, jnp.where(oh1, 1.0, 0.0))
    tok = lax.broadcasted_iota(jnp.int32, (N_EXPERTS, tm), 1)
    ct = oh
    span = 1
    while span < tm:
        ct = ct + jnp.where(tok >= span, pltpu.roll(ct, span, 1), 0.0)
        span *= 2
    ct = ct - oh
    n_col = jnp.sum(oh, axis=1, keepdims=True)
    nch_col = jnp.floor((n_col + (CHUNK - 1)) * (1.0 / CHUNK))
    e_sub = lax.broadcasted_iota(jnp.int32, (N_EXPERTS, LANES), 0)
    l_lane = lax.broadcasted_iota(jnp.int32, (N_EXPERTS, LANES), 1)
    nch_row = jnp.sum(jnp.where(e_sub == l_lane, nch_col, 0.0), axis=0, keepdims=True)
    coff_col = jnp.sum(jnp.where(l_lane < e_sub, nch_row, 0.0), axis=1, keepdims=True)
    coff_row = jnp.sum(jnp.where(e_sub == l_lane, coff_col, 0.0), axis=0, keepdims=True)
    info_v_ref[...] = jnp.concatenate(
        [nch_row, coff_row, jnp.zeros((SUBLANES - 2, LANES), jnp.float32)], axis=0
    ).astype(jnp.int32)
    info_copy(prv).start()

    posbase = ct + coff_col * CHUNK
    pos0 = jnp.sum(jnp.where(oh0, posbase, 0.0), axis=0, keepdims=True)
    pos1 = jnp.sum(jnp.where(oh1, posbase, 0.0), axis=0, keepdims=True)
    route_t = jnp.concatenate(
        [pos0, pos1, gate0, gate1, jnp.zeros((LANES - 4, tm), jnp.float32)], axis=0)
    route_ref[...] = route_t.T

    pos0i = pos0.astype(jnp.int32)
    pos1i = pos1.astype(jnp.int32)
    for r0 in range(0, STAGE_ROWS, PERM_ROWS):
        r = lax.broadcasted_iota(jnp.int32, (PERM_ROWS, tm), 0) + r0
        perm_ref[r0:r0 + PERM_ROWS, :] = jnp.where(
            r == pos0i, 1.0, jnp.where(r == pos1i, 1.0, 0.0)).astype(jnp.bfloat16)

    x = x_ref[...]
    proj_vv = jnp.dot(hn, win_ref[:, 2 * d_conv:], preferred_element_type=jnp.float32)
    b_gate = proj_bc[:, 0:d_conv]
    v_pool = proj_vv[:, d_conv:]
    extu_ref[HALO:, :] = proj_bc[:, d_conv:] * proj_vv[:, 0:d_conv]
    extp_ref[HALO:, :] = v_pool

    hn_next = _rms(xn_ref[...], nmix).astype(jnp.bfloat16)
    anchor = proj_vv[tm - 1:tm, :] + hn_next[tm - 1:tm, :].astype(jnp.float32)
    hf_late = hfp + _derived_zero(anchor).astype(jnp.bfloat16)
    for r0 in range(0, STAGE_ROWS, PERM_ROWS):
        rows = jnp.dot(perm_ref[r0:r0 + PERM_ROWS, :], hf_late, preferred_element_type=jnp.float32)
        stage_ref[st_sort, r0 // CHUNK:(r0 + PERM_ROWS) // CHUNK] = (
            rows.astype(jnp.bfloat16).reshape(PERM_ROWS // CHUNK, CHUNK, x_ref.shape[-1]))

    conv = convw_ref[CONV_WIDTH - 1:CONV_WIDTH, :] * extu_ref[HALO:, :]
    for k in range(CONV_WIDTH - 1):
        off = HALO - (CONV_WIDTH - 1 - k)
        conv = conv + convw_ref[k:k + 1, :] * extu_ref[off:off + tm, :]
    y_conv = _rms(b_gate * conv, nconv_ref[...])

    mixed = []
    for g, w in enumerate(POOL_WINDOWS):
        cols = slice(g * LANES, (g + 1) * LANES)
        acc = extp_ref[:, cols]
        span = 1
        while span < w:
            acc = acc + pltpu.roll(acc, span, 0)
            span *= 2
        pooled = acc[HALO:] * (1.0 / w) - v_pool[:, cols]
        mixed.append(jnp.dot(pooled.astype(jnp.bfloat16), poolw_ref[g],
                             preferred_element_type=jnp.float32))
    y_pool = _rms(jnp.concatenate(mixed, axis=-1) * pscale_ref[...], npool_ref[...])

    y = jnp.concatenate([y_conv, y_pool], axis=-1).astype(jnp.bfloat16)
    h1 = x + jnp.dot(y, wout_ref[...], preferred_element_type=jnp.float32)
    h1_ref[...] = h1
    h1s_ref[...] = h1
    hn_ref[1] = hn_next

    extu_ref[0:HALO, :] = extu_ref[tm:tm + HALO, :]
    extp_ref[0:HALO, :] = extp_ref[tm:tm + HALO, :]

    pending = pend_ref[0]
    tile2 = jnp.maximum(s - 2, 0)
    tot = 0
    issued = 0
    most = 0
    plan = []
    for e in range(N_EXPERTS):
        nch = info_s_ref[cur, 0, e]
        base = run_ref[e]
        tab_ref[tile2 * TAB_STRIDE + e] = base
        tab_ref[tile2 * TAB_STRIDE + N_EXPERTS + e] = nch
        run_ref[e] = base + nch
        plan.append((nch > 0, info_s_ref[cur, 1, e], e * cap_chunks + base))
        tot = tot + nch
        issued = issued + jnp.where(nch > 0, SEGMENT_CHUNKS, 0) + jnp.maximum(nch - SEGMENT_CHUNKS, 0)
        most = jnp.maximum(most, nch)
    tab_ref[tile2 * TAB_STRIDE + 2 * N_EXPERTS] = tot
    pend_ref[0] = issued
    _drain(pending, zero_ref, xs_ref, sem_out)
    for nonempty, src0, dst0 in plan:
        @pl.when(nonempty)
        def _():
            _copy_chunks(stage_ref.at[st_disp], src0, xs_ref, dst0, sem_out, SEGMENT_CHUNKS).start()

    @pl.when(most > SEGMENT_CHUNKS)
    def _():
        def body(e, c):
            _issue_rest(SEGMENT_CHUNKS, info_s_ref[cur, 0, e],
                        stage_ref.at[st_disp], info_s_ref[cur, 1, e],
                        xs_ref, e * cap_chunks + tab_ref[tile2 * TAB_STRIDE + e], sem_out)
            return c
        lax.fori_loop(0, N_EXPERTS, body, 0)

    @pl.when(s == n_tiles)
    def _():
        info_copy(prv).wait()
        _drain(pend_ref[0], zero_ref, xs_ref, sem_out)
        sem_last = sem_out

        def last_body(e, tot):
            nch = info_s_ref[prv, 0, e]
            base = run_ref[e]
            tab_ref[(s - 1) * TAB_STRIDE + e] = base
            tab_ref[(s - 1) * TAB_STRIDE + N_EXPERTS + e] = nch
            _issue_rest(0, nch, stage_ref.at[st_sort], info_s_ref[prv, 1, e],
                        xs_ref, e * cap_chunks + base, sem_last)
            run_ref[e] = base + nch
            return tot + nch
        last = lax.fori_loop(0, N_EXPERTS, last_body, 0)
        tab_ref[(s - 1) * TAB_STRIDE + 2 * N_EXPERTS] = last
        _drain(last, zero_ref, xs_ref, sem_last)

        def pad_body(e, carry):
            tot, w = carry
            n = run_ref[e]
            tiles = lax.shift_right_logical(n + (MOE_CHUNKS - 1), MOE_CHUNKS_LOG2)
            nch = tiles * MOE_CHUNKS - n

            def chunk_body(k, c):
                _copy_chunks(zero_ref, 0, xs_ref, e * cap_chunks + n + k, sem_last).start()
                return c
            lax.fori_loop(0, nch, chunk_body, 0)

            def work_body(j, c):
                wexp_ref[w + j] = e
                wblk_ref[w + j] = e * (cap_chunks // MOE_CHUNKS) + j
                return c
            lax.fori_loop(0, tiles, work_body, 0)
            return tot + nch, w + tiles
        n_pad, n_work = lax.fori_loop(0, N_EXPERTS, pad_body, (0, 0))
        _drain(n_pad, zero_ref, xs_ref, sem_last)
        nwork_ref[0] = n_work

        def fill_body(w, c):
            wexp_ref[w] = wexp_ref[n_work - 1]
            wblk_ref[w] = wblk_ref[n_work - 1]
            return c
        lax.fori_loop(n_work, max_work, fill_body, 0)


def _moe_kernel(exp_ref, blk_ref, n_ref, xs_ref, wg_ref, wu_ref, wd_ref, ys_ref,
                wgu_bf_ref, wd_bf_ref, *, d_exp):
    w = pl.program_id(0)

    @pl.when((w == 0) | (exp_ref[w] != exp_ref[jnp.maximum(w - 1, 0)]))
    def _():
        wgu_bf_ref[:, :d_exp] = wg_ref[0].astype(jnp.bfloat16)
        wgu_bf_ref[:, d_exp:] = wu_ref[0].astype(jnp.bfloat16)
        wd_bf_ref[...] = wd_ref[0].astype(jnp.bfloat16)

    @pl.when(w < n_ref[0])
    def _():
        gu = jnp.dot(xs_ref[...], wgu_bf_ref[...], preferred_element_type=jnp.float32)
        g = gu[:, :d_exp]
        a = g * jax.nn.sigmoid(g) * gu[:, d_exp:]
        ys_ref[...] = jnp.dot(a.astype(jnp.bfloat16), wd_bf_ref[...],
                              preferred_element_type=jnp.float32).astype(jnp.bfloat16)


def _combine_kernel(tab_ref, h1_ref, route_ref, fn_ref, ys_ref, out_ref, stage_ref, sem,
                    *, tm, cap_chunks):
    s = pl.program_id(0)
    n_steps = pl.num_programs(0)
    slot = s % 3
    ahead = (s + 2) % 3

    @pl.when(s == 0)
    def _():
        stage_ref[...] = jnp.zeros_like(stage_ref)
        for j in range(2):
            def body(e, coff):
                nch = tab_ref[j * TAB_STRIDE + N_EXPERTS + e]
                _issue_rest(0, nch, ys_ref, e * cap_chunks + tab_ref[j * TAB_STRIDE + e],
                            stage_ref.at[j], coff, sem.at[j])
                return coff + nch
            lax.fori_loop(0, N_EXPERTS, body, 0)

    _drain(tab_ref[s * TAB_STRIDE + 2 * N_EXPERTS], ys_ref, stage_ref.at[slot], sem.at[slot])

    route = route_ref[...]
    pos0 = route[:, 0:1].astype(jnp.int32)
    pos1 = route[:, 1:2].astype(jnp.int32)
    gate0 = route[:, 2:3]
    gate1 = route[:, 3:4]
    r = lax.broadcasted_iota(jnp.int32, (tm, STAGE_ROWS), 1)
    pw = jnp.where(r == pos0, gate0, jnp.where(r == pos1, gate1, 0.0)).astype(jnp.bfloat16)
    moe = jnp.dot(pw, stage_ref[slot].reshape(STAGE_ROWS, out_ref.shape[-1]),
                  preferred_element_type=jnp.float32)
    out_ref[...] = _rms(h1_ref[...] + moe, fn_ref[...])

    nxt = jnp.minimum(s + 2, n_steps - 1)
    has_next = (s + 2 < n_steps).astype(jnp.int32)
    coff = 0
    most = 0
    plan = []
    for e in range(N_EXPERTS):
        nch = tab_ref[nxt * TAB_STRIDE + N_EXPERTS + e] * has_next
        plan.append((jnp.minimum(nch, UNROLLED_CHUNKS),
                     e * cap_chunks + tab_ref[nxt * TAB_STRIDE + e], coff))
        coff = coff + nch
        most = jnp.maximum(most, nch)
    for head, src0, dst0 in plan:
        _issue_head(head, ys_ref, src0, stage_ref.at[ahead], dst0, sem.at[ahead])

    @pl.when(most > UNROLLED_CHUNKS)
    def _():
        def body(e, coff):
            nch = tab_ref[nxt * TAB_STRIDE + N_EXPERTS + e]
            _issue_rest(UNROLLED_CHUNKS, nch, ys_ref, e * cap_chunks + tab_ref[nxt * TAB_STRIDE + e],
                        stage_ref.at[ahead], coff, sem.at[ahead])
            return coff + nch
        lax.fori_loop(0, N_EXPERTS, body, 0)


def kernel(x, meta_tokens, norm_mix, w_in, conv_w, norm_conv_out, pool_w, pool_scale,
           norm_pool_out, w_out, norm_ffn, w_router_group, w_router_expert,
           w_gate, w_up, w_down, final_norm):
    bsz, seq, d = x.shape
    d_conv = conv_w.shape[-1]
    d_exp = w_gate.shape[-1]
    n_exp = w_gate.shape[1]
    assert meta_tokens.shape[0] == N_META == HALO and n_exp == N_EXPERTS
    assert seq % MIX_TILE == 0
    bf = jnp.bfloat16
    i32 = jnp.int32
    t = bsz * seq
    n_tiles = t // MIX_TILE
    cap = -(-(t + CHUNK * n_tiles) // MOE_TILE) * MOE_TILE + MOE_TILE
    n_chunks = N_EXPERTS * cap // CHUNK
    max_work = (2 * t + N_EXPERTS * (CHUNK - 1) * n_tiles) // MOE_TILE + N_EXPERTS
    row = lambda a: a.reshape(1, -1)
    wr = jnp.concatenate(
        [w_router_expert[0], w_router_group[0],
         jnp.zeros((d, LANES - N_EXPERTS - N_EXPERT_GROUPS), jnp.float32)], axis=1).astype(bf)

    full = lambda shape: pl.BlockSpec(shape, lambda s: (0,) * len(shape))
    last = n_tiles - 1
    smem = pl.BlockSpec(memory_space=pltpu.SMEM)
    x2 = x.reshape(t, d)
    h1, route, xs, tab, wexp, wblk, nwork = pl.pallas_call(
        functools.partial(_mixer_kernel, tm=MIX_TILE, d_conv=d_conv, cap_chunks=cap // CHUNK,
                          tiles_per_seq=seq // MIX_TILE, max_work=max_work),
        grid=(n_tiles + 1,),
        in_specs=[pl.BlockSpec((MIX_TILE, d), lambda s: (jnp.minimum(s, last), 0)),
                  pl.BlockSpec((MIX_TILE, d), lambda s: (jnp.minimum(s + 1, last), 0)),
                  full((N_META, d)), full((1, d)), full((d, 4 * d_conv)),
                  full((CONV_WIDTH, d_conv)), full((1, d_conv)),
                  full((len(POOL_WINDOWS), LANES, LANES)), full((1, d_conv)), full((1, d_conv)),
                  full((2 * d_conv, d)), full((1, d)), full((d, LANES))],
        out_specs=[pl.BlockSpec((MIX_TILE, d), lambda s: (s, 0)),
                   pl.BlockSpec((MIX_TILE, LANES), lambda s: (jnp.maximum(s - 1, 0), 0)),
                   pl.BlockSpec(memory_space=pl.ANY), smem, smem, smem, smem],
        out_shape=[jax.ShapeDtypeStruct((t + MIX_TILE, d), jnp.float32),
                   jax.ShapeDtypeStruct((t, LANES), jnp.float32),
                   jax.ShapeDtypeStruct((n_chunks, CHUNK, d), bf),
                   jax.ShapeDtypeStruct((n_tiles * TAB_STRIDE,), i32),
                   jax.ShapeDtypeStruct((max_work,), i32),
                   jax.ShapeDtypeStruct((max_work,), i32),
                   jax.ShapeDtypeStruct((1,), i32)],
        scratch_shapes=[pltpu.VMEM((MIX_TILE + HALO, d_conv), jnp.float32),
                        pltpu.VMEM((MIX_TILE + HALO, d_conv), jnp.float32),
                        pltpu.VMEM((2, MIX_TILE, d), bf),
                        pltpu.VMEM((MIX_TILE, d), jnp.float32),
                        pltpu.VMEM((STAGE_ROWS, MIX_TILE), bf),
                        pltpu.VMEM((3, STAGE_CHUNKS + SEGMENT_CHUNKS, CHUNK, d), bf),
                        pltpu.VMEM((WAIT_GROUP, CHUNK, d), bf),
                        pltpu.VMEM((SUBLANES, LANES), i32),
                        pltpu.SMEM((2, SUBLANES, LANES), i32),
                        pltpu.SMEM((N_EXPERTS,), i32),
                        pltpu.SMEM((1,), i32),
                        pltpu.SemaphoreType.DMA,
                        pltpu.SemaphoreType.DMA],
        compiler_params=pltpu.CompilerParams(
            dimension_semantics=("arbitrary",), vmem_limit_bytes=VMEM_LIMIT),
        name="mixer",
    )(x2, x2, meta_tokens, row(norm_mix), w_in[0].astype(bf), conv_w[0],
      row(norm_conv_out), pool_w[0].astype(bf), row(pool_scale), row(norm_pool_out),
      w_out[0].astype(bf), row(norm_ffn), wr)

    ys = pl.pallas_call(
        functools.partial(_moe_kernel, d_exp=d_exp),
        grid_spec=pltpu.PrefetchScalarGridSpec(
            num_scalar_prefetch=3,
            grid=(max_work,),
            in_specs=[pl.BlockSpec((MOE_TILE, d), lambda w, ex, bk, n: (bk[w], 0)),
                      pl.BlockSpec((1, d, d_exp), lambda w, ex, bk, n: (ex[w], 0, 0)),
                      pl.BlockSpec((1, d, d_exp), lambda w, ex, bk, n: (ex[w], 0, 0)),
                      pl.BlockSpec((1, d_exp, d), lambda w, ex, bk, n: (ex[w], 0, 0))],
            out_specs=pl.BlockSpec((MOE_TILE, d), lambda w, ex, bk, n: (bk[w], 0)),
            scratch_shapes=[pltpu.VMEM((d, 2 * d_exp), bf), pltpu.VMEM((d_exp, d), bf)]),
        out_shape=jax.ShapeDtypeStruct((n_chunks * CHUNK, d), bf),
        compiler_params=pltpu.CompilerParams(
            dimension_semantics=("arbitrary",), vmem_limit_bytes=VMEM_LIMIT),
        name="moe",
    )(wexp, wblk, nwork, xs.reshape(n_chunks * CHUNK, d), w_gate[0], w_up[0], w_down[0])

    out = pl.pallas_call(
        functools.partial(_combine_kernel, tm=MIX_TILE, cap_chunks=cap // CHUNK),
        grid_spec=pltpu.PrefetchScalarGridSpec(
            num_scalar_prefetch=1,
            grid=(n_tiles,),
            in_specs=[pl.BlockSpec((MIX_TILE, d), lambda s, tb: (s, 0)),
                      pl.BlockSpec((MIX_TILE, LANES), lambda s, tb: (s, 0)),
                      pl.BlockSpec((1, d), lambda s, tb: (0, 0)),
                      pl.BlockSpec(memory_space=pl.ANY)],
            out_specs=pl.BlockSpec((MIX_TILE, d), lambda s, tb: (s, 0)),
            scratch_shapes=[pltpu.VMEM((3, STAGE_CHUNKS, CHUNK, d), bf),
                            pltpu.SemaphoreType.DMA((3,))]),
        out_shape=jax.ShapeDtypeStruct((t, d), jnp.float32),
        compiler_params=pltpu.CompilerParams(
            dimension_semantics=("arbitrary",), vmem_limit_bytes=VMEM_LIMIT),
        name="combine",
    )(tab, h1, route, row(final_norm), ys.reshape(n_chunks, CHUNK, d))
    return out.reshape(bsz, seq, d)
```

```python
import functools

import jax
import jax.numpy as jnp
from jax import lax
from jax.experimental import pallas as pl
from jax.experimental.pallas import tpu as pltpu

EPS = 1e-6
N_META = 16
CONV_WIDTH = 3
POOL_WINDOWS = (2, 4, 8, 16)
N_EXPERT_GROUPS = 4
EXPERTS_PER_GROUP = 8
N_EXPERTS = N_EXPERT_GROUPS * EXPERTS_PER_GROUP
LANES = 128
SUBLANES = 8
HALO = 16
CHUNK = 16

MIX_TILE = 512
MOE_TILE = 1024
MOE_CHUNKS = MOE_TILE // CHUNK
MOE_CHUNKS_LOG2 = MOE_CHUNKS.bit_length() - 1
VMEM_LIMIT = 56 * 1024 * 1024

STAGE_ROWS = 2 * MIX_TILE + N_EXPERTS * CHUNK
STAGE_CHUNKS = STAGE_ROWS // CHUNK
PERM_ROWS = 512
TAB_STRIDE = 2 * N_EXPERTS + 1
UNROLLED_CHUNKS = 5
SEGMENT_CHUNKS = 5
WAIT_GROUP = 16
WAIT_GROUP_LOG2 = WAIT_GROUP.bit_length() - 1


def _rms(x, g):
    return x * lax.rsqrt(jnp.mean(x * x, axis=-1, keepdims=True) + EPS) * g


def _route(lt):
    tm = lt.shape[1]
    sub = lax.broadcasted_iota(jnp.int32, (SUBLANES, tm), 0)
    neg = jnp.float32(-jnp.inf)
    big = jnp.int32(1 << 20)
    lg = jnp.where(sub < N_EXPERT_GROUPS, lt[N_EXPERTS:N_EXPERTS + SUBLANES], neg)
    gmax = jnp.max(lg, axis=0, keepdims=True)
    g1 = 1.0 / jnp.sum(jnp.exp(lg - gmax), axis=0, keepdims=True)
    gsel = jnp.min(jnp.where(lg == gmax, sub, big), axis=0, keepdims=True)
    le = lt[(N_EXPERT_GROUPS - 1) * SUBLANES:N_EXPERT_GROUPS * SUBLANES]
    for g in range(N_EXPERT_GROUPS - 2, -1, -1):
        le = jnp.where(gsel == g, lt[g * SUBLANES:(g + 1) * SUBLANES], le)
    m1 = jnp.max(le, axis=0, keepdims=True)
    pe = jnp.exp(le - m1)
    i1 = jnp.min(jnp.where(le == m1, sub, big), axis=0, keepdims=True)
    pe2 = jnp.where(sub == i1, -1.0, pe)
    p2 = jnp.max(pe2, axis=0, keepdims=True)
    i2 = jnp.min(jnp.where(pe2 == p2, sub, big), axis=0, keepdims=True)
    inv = 1.0 / (1.0 + p2)
    return (gsel * EXPERTS_PER_GROUP + i1, gsel * EXPERTS_PER_GROUP + i2,
            g1 * inv, g1 * p2 * inv)


def _derived_zero(x):
    bits = pltpu.bitcast(x, jnp.int32)
    return lax.shift_right_logical(lax.shift_right_logical(bits, 16), 16).astype(jnp.float32)


def _copy_chunks(src_ref, src_chunk, dst_ref, dst_chunk, sem, n=1):
    return pltpu.make_async_copy(src_ref.at[pl.ds(src_chunk, n)],
                                 dst_ref.at[pl.ds(dst_chunk, n)], sem)


def _drain(n_chunks, src_ref, dst_ref, sem):
    def group_body(k, c):
        _copy_chunks(src_ref, 0, dst_ref, 0, sem, WAIT_GROUP).wait()
        return c
    lax.fori_loop(0, lax.shift_right_logical(n_chunks, WAIT_GROUP_LOG2), group_body, 0)

    def one_body(k, c):
        _copy_chunks(src_ref, 0, dst_ref, 0, sem).wait()
        return c
    lax.fori_loop(0, n_chunks & (WAIT_GROUP - 1), one_body, 0)


def _issue_head(head, src_ref, src0, dst_ref, dst0, sem):
    pairs = lax.shift_right_logical(head, 1)
    for k in range(UNROLLED_CHUNKS // 2):
        @pl.when(k < pairs)
        def _():
            _copy_chunks(src_ref, src0 + 2 * k, dst_ref, dst0 + 2 * k, sem, 2).start()

    @pl.when((head & 1) == 1)
    def _():
        _copy_chunks(src_ref, src0 + 2 * pairs, dst_ref, dst0 + 2 * pairs, sem).start()


def _issue_rest(first, nch, src_ref, src0, dst_ref, dst0, sem):
    def body(k, c):
        _copy_chunks(src_ref, src0 + k, dst_ref, dst0 + k, sem).start()
        return c
    lax.fori_loop(first, nch, body, 0)


def _mixer_kernel(x_ref, xn_ref, meta_ref, nmix_ref, win_ref, convw_ref, nconv_ref, poolw_ref,
                  pscale_ref, npool_ref, wout_ref, nffn_ref, wr_ref,
                  h1_ref, route_ref, xs_ref, tab_ref, wexp_ref, wblk_ref, wnxt_ref, nwork_ref,
                  extu_ref, extp_ref, hn_ref, h1s_ref, perm_ref, stage_ref, zero_ref, info_v_ref,
                  info_s_ref, run_ref, pend_ref, sem_info, sem_out,
                  *, tm, d_conv, cap_chunks, tiles_per_seq, max_work):
    s = pl.program_id(0)
    n_tiles = pl.num_programs(0) - 1
    cur = s % 2
    prv = 1 - cur
    st_sort = (s + 2) % 3
    st_disp = (s + 1) % 3
    nmix = nmix_ref[...]

    def info_copy(slot):
        return pltpu.make_async_copy(info_v_ref, info_s_ref.at[slot], sem_info)

    @pl.when(s == 0)
    def _():
        pend_ref[0] = 0
        zero_ref[...] = jnp.zeros_like(zero_ref)
        stage_ref[...] = jnp.zeros_like(stage_ref)
        h1s_ref[...] = jnp.zeros_like(h1s_ref)
        hn_ref[1] = _rms(x_ref[...], nmix).astype(jnp.bfloat16)
        info_v_ref[...] = jnp.zeros_like(info_v_ref)
        info_copy(0).start()
        info_copy(0).wait()
        info_copy(1).start()

        def init(e, c):
            run_ref[e] = 0
            return c
        lax.fori_loop(0, N_EXPERTS, init, 0)

    @pl.when(s % tiles_per_seq == 0)
    def _():
        hm = _rms(meta_ref[...], nmix).astype(jnp.bfloat16)
        pm = jnp.dot(hm, win_ref[:, d_conv:], preferred_element_type=jnp.float32)
        extu_ref[0:HALO, :] = pm[:, 0:d_conv] * pm[:, d_conv:2 * d_conv]
        extp_ref[0:HALO, :] = pm[:, 2 * d_conv:]

    info_copy(cur).wait()

    hn_ref[0] = hn_ref[1]
    hn = hn_ref[0]
    proj_bc = jnp.dot(hn, win_ref[:, 0:2 * d_conv], preferred_element_type=jnp.float32)

    hfp = _rms(h1s_ref[...], nffn_ref[...]).astype(jnp.bfloat16)
    lt = jnp.dot(hfp, wr_ref[...], preferred_element_type=jnp.float32).T
    e0, e1, gate0, gate1 = _route(lt)
    esub = lax.broadcasted_iota(jnp.int32, (N_EXPERTS, tm), 0)
    dead = (s == 0).astype(jnp.int32)
    oh0 = esub == e0 - dead * N_EXPERTS
    oh1 = esub == e1 - dead * N_EXPERTS
    oh = jnp.where(oh0, 1.0, jnp.where(oh1, 1.0, 0.0))
    tok = lax.broadcasted_iota(jnp.int32, (N_EXPERTS, tm), 1)
    ct = oh
    span = 1
    while span < tm:
        ct = ct + jnp.where(tok >= span, pltpu.roll(ct, span, 1), 0.0)
        span *= 2
    ct = ct - oh
    n_col = jnp.sum(oh, axis=1, keepdims=True)
    nch_col = jnp.floor((n_col + (CHUNK - 1)) * (1.0 / CHUNK))
    e_sub = lax.broadcasted_iota(jnp.int32, (N_EXPERTS, LANES), 0)
    l_lane = lax.broadcasted_iota(jnp.int32, (N_EXPERTS, LANES), 1)
    nch_row = jnp.sum(jnp.where(e_sub == l_lane, nch_col, 0.0), axis=0, keepdims=True)
    coff_col = jnp.sum(jnp.where(l_lane < e_sub, nch_row, 0.0), axis=1, keepdims=True)
    coff_row = jnp.sum(jnp.where(e_sub == l_lane, coff_col, 0.0), axis=0, keepdims=True)
    info_v_ref[...] = jnp.concatenate(
        [nch_row, coff_row, jnp.zeros((SUBLANES - 2, LANES), jnp.float32)], axis=0
    ).astype(jnp.int32)
    info_copy(prv).start()

    posbase = ct + coff_col * CHUNK
    pos0 = jnp.sum(jnp.where(oh0, posbase, 0.0), axis=0, keepdims=True)
    pos1 = jnp.sum(jnp.where(oh1, posbase, 0.0), axis=0, keepdims=True)
    route_t = jnp.concatenate(
        [pos0, pos1, gate0, gate1, jnp.zeros((LANES - 4, tm), jnp.float32)], axis=0)
    route_ref[...] = route_t.T

    pos0i = pos0.astype(jnp.int32)
    pos1i = pos1.astype(jnp.int32)
    for r0 in range(0, STAGE_ROWS, PERM_ROWS):
        r = lax.broadcasted_iota(jnp.int32, (PERM_ROWS, tm), 0) + r0
        perm_ref[r0:r0 + PERM_ROWS, :] = jnp.where(
            r == pos0i, 1.0, jnp.where(r == pos1i, 1.0, 0.0)).astype(jnp.bfloat16)

    x = x_ref[...]
    proj_vv = jnp.dot(hn, win_ref[:, 2 * d_conv:], preferred_element_type=jnp.float32)
    b_gate = proj_bc[:, 0:d_conv]
    v_pool = proj_vv[:, d_conv:]
    extu_ref[HALO:, :] = proj_bc[:, d_conv:] * proj_vv[:, 0:d_conv]
    extp_ref[HALO:, :] = v_pool

    hn_next = _rms(xn_ref[...], nmix).astype(jnp.bfloat16)
    anchor = proj_vv[tm - 1:tm, :] + hn_next[tm - 1:tm, :].astype(jnp.float32)
    hf_late = hfp + _derived_zero(anchor).astype(jnp.bfloat16)
    for r0 in range(0, STAGE_ROWS, PERM_ROWS):
        rows = jnp.dot(perm_ref[r0:r0 + PERM_ROWS, :], hf_late, preferred_element_type=jnp.float32)
        stage_ref[st_sort, r0 // CHUNK:(r0 + PERM_ROWS) // CHUNK] = (
            rows.astype(jnp.bfloat16).reshape(PERM_ROWS // CHUNK, CHUNK, x_ref.shape[-1]))

    conv = convw_ref[CONV_WIDTH - 1:CONV_WIDTH, :] * extu_ref[HALO:, :]
    for k in range(CONV_WIDTH - 1):
        off = HALO - (CONV_WIDTH - 1 - k)
        conv = conv + convw_ref[k:k + 1, :] * extu_ref[off:off + tm, :]
    y_conv = _rms(b_gate * conv, nconv_ref[...])

    mixed = []
    for g, w in enumerate(POOL_WINDOWS):
        cols = slice(g * LANES, (g + 1) * LANES)
        acc = extp_ref[:, cols]
        span = 1
        while span < w:
            acc = acc + pltpu.roll(acc, span, 0)
            span *= 2
        pooled = acc[HALO:] * (1.0 / w) - v_pool[:, cols]
        mixed.append(jnp.dot(pooled.astype(jnp.bfloat16), poolw_ref[g],
                             preferred_element_type=jnp.float32))
    y_pool = _rms(jnp.concatenate(mixed, axis=-1) * pscale_ref[...], npool_ref[...])

    y = jnp.concatenate([y_conv, y_pool], axis=-1).astype(jnp.bfloat16)
    h1 = x + jnp.dot(y, wout_ref[...], preferred_element_type=jnp.float32)
    h1_ref[...] = h1
    h1s_ref[...] = h1
    hn_ref[1] = hn_next

    extu_ref[0:HALO, :] = extu_ref[tm:tm + HALO, :]
    extp_ref[0:HALO, :] = extp_ref[tm:tm + HALO, :]

    pending = pend_ref[0]
    tile2 = jnp.maximum(s - 2, 0)
    tot = 0
    issued = 0
    most = 0
    plan = []
    for e in range(N_EXPERTS):
        nch = info_s_ref[cur, 0, e]
        base = run_ref[e]
        tab_ref[tile2 * TAB_STRIDE + e] = base
        tab_ref[tile2 * TAB_STRIDE + N_EXPERTS + e] = nch
        run_ref[e] = base + nch
        plan.append((nch > 0, info_s_ref[cur, 1, e], e * cap_chunks + base))
        tot = tot + nch
        issued = issued + jnp.where(nch > 0, SEGMENT_CHUNKS, 0) + jnp.maximum(nch - SEGMENT_CHUNKS, 0)
        most = jnp.maximum(most, nch)
    tab_ref[tile2 * TAB_STRIDE + 2 * N_EXPERTS] = tot
    pend_ref[0] = issued
    _drain(pending, zero_ref, xs_ref, sem_out)
    for nonempty, src0, dst0 in plan:
        @pl.when(nonempty)
        def _():
            _copy_chunks(stage_ref.at[st_disp], src0, xs_ref, dst0, sem_out, SEGMENT_CHUNKS).start()

    @pl.when(most > SEGMENT_CHUNKS)
    def _():
        def body(e, c):
            _issue_rest(SEGMENT_CHUNKS, info_s_ref[cur, 0, e],
                        stage_ref.at[st_disp], info_s_ref[cur, 1, e],
                        xs_ref, e * cap_chunks + tab_ref[tile2 * TAB_STRIDE + e], sem_out)
            return c
        lax.fori_loop(0, N_EXPERTS, body, 0)

    @pl.when(s == n_tiles)
    def _():
        info_copy(prv).wait()
        _drain(pend_ref[0], zero_ref, xs_ref, sem_out)
        sem_last = sem_out

        def last_body(e, tot):
            nch = info_s_ref[prv, 0, e]
            base = run_ref[e]
            tab_ref[(s - 1) * TAB_STRIDE + e] = base
            tab_ref[(s - 1) * TAB_STRIDE + N_EXPERTS + e] = nch
            _issue_rest(0, nch, stage_ref.at[st_sort], info_s_ref[prv, 1, e],
                        xs_ref, e * cap_chunks + base, sem_last)
            run_ref[e] = base + nch
            return tot + nch
        last = lax.fori_loop(0, N_EXPERTS, last_body, 0)
        tab_ref[(s - 1) * TAB_STRIDE + 2 * N_EXPERTS] = last
        _drain(last, zero_ref, xs_ref, sem_last)

        def pad_body(e, carry):
            tot, w = carry
            n = run_ref[e]
            tiles = lax.shift_right_logical(n + (MOE_CHUNKS - 1), MOE_CHUNKS_LOG2)
            nch = tiles * MOE_CHUNKS - n

            def chunk_body(k, c):
                _copy_chunks(zero_ref, 0, xs_ref, e * cap_chunks + n + k, sem_last).start()
                return c
            lax.fori_loop(0, nch, chunk_body, 0)

            def work_body(j, c):
                wexp_ref[w + j] = e
                wblk_ref[w + j] = e * (cap_chunks // MOE_CHUNKS) + j
                return c
            lax.fori_loop(0, tiles, work_body, 0)
            return tot + nch, w + tiles
        n_pad, n_work = lax.fori_loop(0, N_EXPERTS, pad_body, (0, 0))
        _drain(n_pad, zero_ref, xs_ref, sem_last)
        nwork_ref[0] = n_work

        def next_body(k, nxt):
            w = n_work - 1 - k
            wnxt_ref[w] = nxt
            prev = wexp_ref[jnp.maximum(w - 1, 0)]
            return jnp.where(prev != wexp_ref[w], wexp_ref[w], nxt)
        lax.fori_loop(0, n_work, next_body, wexp_ref[n_work - 1])

        def fill_body(w, c):
            wexp_ref[w] = 0
            wblk_ref[w] = 0
            wnxt_ref[w] = 0
            return c
        lax.fori_loop(n_work, max_work, fill_body, 0)


def _moe_kernel(exp_ref, blk_ref, nxt_ref, n_ref, xs_ref, wg_ref, wu_ref, wd_ref, ys_ref,
                xbuf_ref, ybuf_ref, wgf_ref, wuf_ref, wdf_ref, wgu_bf_ref, wd_bf_ref,
                sem_x, sem_y, sem_w, *, d_exp, tile):
    n = n_ref[0]

    def x_copy(w, slot):
        return pltpu.make_async_copy(xs_ref.at[pl.ds(blk_ref[w] * tile, tile)],
                                     xbuf_ref.at[slot], sem_x.at[slot])

    def y_copy(w, slot):
        return pltpu.make_async_copy(ybuf_ref.at[slot],
                                     ys_ref.at[pl.ds(blk_ref[w] * tile, tile)], sem_y.at[slot])

    def w_copies(e, slot):
        return (pltpu.make_async_copy(wg_ref.at[e], wgf_ref.at[slot], sem_w.at[slot]),
                pltpu.make_async_copy(wu_ref.at[e], wuf_ref.at[slot], sem_w.at[slot]),
                pltpu.make_async_copy(wd_ref.at[e], wdf_ref.at[slot], sem_w.at[slot]))

    for c in w_copies(exp_ref[0], 0):
        c.start()
    x_copy(0, 0).start()

    @pl.when(n > 1)
    def _():
        x_copy(1, 1).start()

    def item(w, run):
        e = exp_ref[w]
        first = (w == 0) | (exp_ref[jnp.maximum(w - 1, 0)] != e)
        wslot = run % 2
        xslot = w % 3
        yslot = w % 2

        @pl.when(first)
        def _():
            for c in w_copies(e, wslot):
                c.wait()
            wgu_bf_ref[:, :d_exp] = wgf_ref[wslot].astype(jnp.bfloat16)
            wgu_bf_ref[:, d_exp:] = wuf_ref[wslot].astype(jnp.bfloat16)
            wd_bf_ref[...] = wdf_ref[wslot].astype(jnp.bfloat16)

            @pl.when(nxt_ref[w] != e)
            def _():
                for c in w_copies(nxt_ref[w], 1 - wslot):
                    c.start()

        x_copy(w, xslot).wait()

        @pl.when(w + 2 < n)
        def _():
            x_copy(w + 2, (w + 2) % 3).start()

        @pl.when(w >= 2)
        def _():
            y_copy(w - 2, yslot).wait()

        gu = jnp.dot(xbuf_ref[xslot], wgu_bf_ref[...], preferred_element_type=jnp.float32)
        g = gu[:, :d_exp]
        a = g * jax.nn.sigmoid(g) * gu[:, d_exp:]
        ybuf_ref[yslot] = jnp.dot(a.astype(jnp.bfloat16), wd_bf_ref[...],
                                  preferred_element_type=jnp.float32).astype(jnp.bfloat16)
        y_copy(w, yslot).start()
        return run + first.astype(jnp.int32)
    lax.fori_loop(0, n, item, 0)

    @pl.when(n >= 2)
    def _():
        y_copy(n - 2, n % 2).wait()
    y_copy(n - 1, (n - 1) % 2).wait()


def _combine_kernel(tab_ref, h1_ref, route_ref, fn_ref, ys_ref, out_ref, stage_ref, sem,
                    *, tm, cap_chunks):
    s = pl.program_id(0)
    n_steps = pl.num_programs(0)
    slot = s % 3
    ahead = (s + 2) % 3

    @pl.when(s == 0)
    def _():
        stage_ref[...] = jnp.zeros_like(stage_ref)
        for j in range(2):
            def body(e, coff):
                nch = tab_ref[j * TAB_STRIDE + N_EXPERTS + e]
                _issue_rest(0, nch, ys_ref, e * cap_chunks + tab_ref[j * TAB_STRIDE + e],
                            stage_ref.at[j], coff, sem.at[j])
                return coff + nch
            lax.fori_loop(0, N_EXPERTS, body, 0)

    _drain(tab_ref[s * TAB_STRIDE + 2 * N_EXPERTS], ys_ref, stage_ref.at[slot], sem.at[slot])

    route = route_ref[...]
    pos0 = route[:, 0:1].astype(jnp.int32)
    pos1 = route[:, 1:2].astype(jnp.int32)
    gate0 = route[:, 2:3]
    gate1 = route[:, 3:4]
    r = lax.broadcasted_iota(jnp.int32, (tm, STAGE_ROWS), 1)
    pw = jnp.where(r == pos0, gate0, jnp.where(r == pos1, gate1, 0.0)).astype(jnp.bfloat16)
    moe = jnp.dot(pw, stage_ref[slot].reshape(STAGE_ROWS, out_ref.shape[-1]),
                  preferred_element_type=jnp.float32)
    out_ref[...] = _rms(h1_ref[...] + moe, fn_ref[...])

    nxt = jnp.minimum(s + 2, n_steps - 1)
    has_next = (s + 2 < n_steps).astype(jnp.int32)
    coff = 0
    most = 0
    plan = []
    for e in range(N_EXPERTS):
        nch = tab_ref[nxt * TAB_STRIDE + N_EXPERTS + e] * has_next
        plan.append((jnp.minimum(nch, UNROLLED_CHUNKS),
                     e * cap_chunks + tab_ref[nxt * TAB_STRIDE + e], coff))
        coff = coff + nch
        most = jnp.maximum(most, nch)
    for head, src0, dst0 in plan:
        _issue_head(head, ys_ref, src0, stage_ref.at[ahead], dst0, sem.at[ahead])

    @pl.when(most > UNROLLED_CHUNKS)
    def _():
        def body(e, coff):
            nch = tab_ref[nxt * TAB_STRIDE + N_EXPERTS + e]
            _issue_rest(UNROLLED_CHUNKS, nch, ys_ref, e * cap_chunks + tab_ref[nxt * TAB_STRIDE + e],
                        stage_ref.at[ahead], coff, sem.at[ahead])
            return coff + nch
        lax.fori_loop(0, N_EXPERTS, body, 0)


def kernel(x, meta_tokens, norm_mix, w_in, conv_w, norm_conv_out, pool_w, pool_scale,
           norm_pool_out, w_out, norm_ffn, w_router_group, w_router_expert,
           w_gate, w_up, w_down, final_norm):
    bsz, seq, d = x.shape
    d_conv = conv_w.shape[-1]
    d_exp = w_gate.shape[-1]
    n_exp = w_gate.shape[1]
    assert meta_tokens.shape[0] == N_META == HALO and n_exp == N_EXPERTS
    assert seq % MIX_TILE == 0
    bf = jnp.bfloat16
    i32 = jnp.int32
    t = bsz * seq
    n_tiles = t // MIX_TILE
    cap = -(-(t + CHUNK * n_tiles) // MOE_TILE) * MOE_TILE + MOE_TILE
    n_chunks = N_EXPERTS * cap // CHUNK
    max_work = (2 * t + N_EXPERTS * (CHUNK - 1) * n_tiles) // MOE_TILE + N_EXPERTS
    row = lambda a: a.reshape(1, -1)
    wr = jnp.concatenate(
        [w_router_expert[0], w_router_group[0],
         jnp.zeros((d, LANES - N_EXPERTS - N_EXPERT_GROUPS), jnp.float32)], axis=1).astype(bf)

    full = lambda shape: pl.BlockSpec(shape, lambda s: (0,) * len(shape))
    last = n_tiles - 1
    smem = pl.BlockSpec(memory_space=pltpu.SMEM)
    x2 = x.reshape(t, d)
    h1, route, xs, tab, wexp, wblk, wnxt, nwork = pl.pallas_call(
        functools.partial(_mixer_kernel, tm=MIX_TILE, d_conv=d_conv, cap_chunks=cap // CHUNK,
                          tiles_per_seq=seq // MIX_TILE, max_work=max_work),
        grid=(n_tiles + 1,),
        in_specs=[pl.BlockSpec((MIX_TILE, d), lambda s: (jnp.minimum(s, last), 0)),
                  pl.BlockSpec((MIX_TILE, d), lambda s: (jnp.minimum(s + 1, last), 0)),
                  full((N_META, d)), full((1, d)), full((d, 4 * d_conv)),
                  full((CONV_WIDTH, d_conv)), full((1, d_conv)),
                  full((len(POOL_WINDOWS), LANES, LANES)), full((1, d_conv)), full((1, d_conv)),
                  full((2 * d_conv, d)), full((1, d)), full((d, LANES))],
        out_specs=[pl.BlockSpec((MIX_TILE, d), lambda s: (s, 0)),
                   pl.BlockSpec((MIX_TILE, LANES), lambda s: (jnp.maximum(s - 1, 0), 0)),
                   pl.BlockSpec(memory_space=pl.ANY), smem, smem, smem, smem, smem],
        out_shape=[jax.ShapeDtypeStruct((t + MIX_TILE, d), jnp.float32),
                   jax.ShapeDtypeStruct((t, LANES), jnp.float32),
                   jax.ShapeDtypeStruct((n_chunks, CHUNK, d), bf),
                   jax.ShapeDtypeStruct((n_tiles * TAB_STRIDE,), i32),
                   jax.ShapeDtypeStruct((max_work,), i32),
                   jax.ShapeDtypeStruct((max_work,), i32),
                   jax.ShapeDtypeStruct((max_work,), i32),
                   jax.ShapeDtypeStruct((1,), i32)],
        scratch_shapes=[pltpu.VMEM((MIX_TILE + HALO, d_conv), jnp.float32),
                        pltpu.VMEM((MIX_TILE + HALO, d_conv), jnp.float32),
                        pltpu.VMEM((2, MIX_TILE, d), bf),
                        pltpu.VMEM((MIX_TILE, d), jnp.float32),
                        pltpu.VMEM((STAGE_ROWS, MIX_TILE), bf),
                        pltpu.VMEM((3, STAGE_CHUNKS + SEGMENT_CHUNKS, CHUNK, d), bf),
                        pltpu.VMEM((WAIT_GROUP, CHUNK, d), bf),
                        pltpu.VMEM((SUBLANES, LANES), i32),
                        pltpu.SMEM((2, SUBLANES, LANES), i32),
                        pltpu.SMEM((N_EXPERTS,), i32),
                        pltpu.SMEM((1,), i32),
                        pltpu.SemaphoreType.DMA,
                        pltpu.SemaphoreType.DMA],
        compiler_params=pltpu.CompilerParams(
            dimension_semantics=("arbitrary",), vmem_limit_bytes=VMEM_LIMIT),
        name="mixer",
    )(x2, x2, meta_tokens, row(norm_mix), w_in[0].astype(bf), conv_w[0],
      row(norm_conv_out), pool_w[0].astype(bf), row(pool_scale), row(norm_pool_out),
      w_out[0].astype(bf), row(norm_ffn), wr)

    anyspec = pl.BlockSpec(memory_space=pl.ANY)
    f32 = jnp.float32
    ys = pl.pallas_call(
        functools.partial(_moe_kernel, d_exp=d_exp, tile=MOE_TILE),
        grid_spec=pltpu.PrefetchScalarGridSpec(
            num_scalar_prefetch=4,
            grid=(1,),
            in_specs=[anyspec, anyspec, anyspec, anyspec],
            out_specs=anyspec,
            scratch_shapes=[pltpu.VMEM((3, MOE_TILE, d), bf), pltpu.VMEM((2, MOE_TILE, d), bf),
                            pltpu.VMEM((2, d, d_exp), f32), pltpu.VMEM((2, d, d_exp), f32),
                            pltpu.VMEM((2, d_exp, d), f32),
                            pltpu.VMEM((d, 2 * d_exp), bf), pltpu.VMEM((d_exp, d), bf),
                            pltpu.SemaphoreType.DMA((3,)), pltpu.SemaphoreType.DMA((2,)),
                            pltpu.SemaphoreType.DMA((2,))]),
        out_shape=jax.ShapeDtypeStruct((n_chunks * CHUNK, d), bf),
        compiler_params=pltpu.CompilerParams(
            dimension_semantics=("arbitrary",), vmem_limit_bytes=VMEM_LIMIT),
        name="moe",
    )(wexp, wblk, wnxt, nwork, xs.reshape(n_chunks * CHUNK, d), w_gate[0], w_up[0], w_down[0])

    out = pl.pallas_call(
        functools.partial(_combine_kernel, tm=MIX_TILE, cap_chunks=cap // CHUNK),
        grid_spec=pltpu.PrefetchScalarGridSpec(
            num_scalar_prefetch=1,
            grid=(n_tiles,),
            in_specs=[pl.BlockSpec((MIX_TILE, d), lambda s, tb: (s, 0)),
                      pl.BlockSpec((MIX_TILE, LANES), lambda s, tb: (s, 0)),
                      pl.BlockSpec((1, d), lambda s, tb: (0, 0)),
                      pl.BlockSpec(memory_space=pl.ANY)],
            out_specs=pl.BlockSpec((MIX_TILE, d), lambda s, tb: (s, 0)),
            scratch_shapes=[pltpu.VMEM((3, STAGE_CHUNKS, CHUNK, d), bf),
                            pltpu.SemaphoreType.DMA((3,))]),
        out_shape=jax.ShapeDtypeStruct((t, d), jnp.float32),
        compiler_params=pltpu.CompilerParams(
            dimension_semantics=("arbitrary",), vmem_limit_bytes=VMEM_LIMIT),
        name="combine",
    )(tab, h1, route, row(final_norm), ys.reshape(n_chunks, CHUNK, d))
    return out.reshape(bsz, seq, d)
```

```python
import functools

import jax
import jax.numpy as jnp
from jax import lax
from jax.experimental import pallas as pl
from jax.experimental.pallas import tpu as pltpu

EPS = 1e-6
N_META = 16
CONV_WIDTH = 3
POOL_WINDOWS = (2, 4, 8, 16)
N_EXPERT_GROUPS = 4
EXPERTS_PER_GROUP = 8
N_EXPERTS = N_EXPERT_GROUPS * EXPERTS_PER_GROUP
LANES = 128
SUBLANES = 8
HALO = 16
CHUNK = 16

MIX_TILE = 512
MOE_TILE = 1024
MOE_SUB = 256
MOE_SUB_CHUNKS = MOE_SUB // CHUNK
MOE_CHUNKS = MOE_TILE // CHUNK
MOE_CHUNKS_LOG2 = MOE_CHUNKS.bit_length() - 1
VMEM_LIMIT = 56 * 1024 * 1024

STAGE_ROWS = 2 * MIX_TILE + N_EXPERTS * CHUNK
STAGE_CHUNKS = STAGE_ROWS // CHUNK
COMMON_ROWS = 2 * MIX_TILE + N_EXPERTS * CHUNK // 2
PERM_ROWS = 512
TAB_STRIDE = 2 * N_EXPERTS + 1
UNROLLED_CHUNKS = 5
SEGMENT_CHUNKS = 5
WAIT_GROUP = 16
WAIT_GROUP_LOG2 = WAIT_GROUP.bit_length() - 1


def _rms(x, g):
    return x * lax.rsqrt(jnp.mean(x * x, axis=-1, keepdims=True) + EPS) * g


def _route(lt):
    tm = lt.shape[1]
    sub = lax.broadcasted_iota(jnp.int32, (SUBLANES, tm), 0)
    neg = jnp.float32(-jnp.inf)
    big = jnp.int32(1 << 20)
    lg = jnp.where(sub < N_EXPERT_GROUPS, lt[N_EXPERTS:N_EXPERTS + SUBLANES], neg)
    gmax = jnp.max(lg, axis=0, keepdims=True)
    g1 = 1.0 / jnp.sum(jnp.exp(lg - gmax), axis=0, keepdims=True)
    gsel = jnp.min(jnp.where(lg == gmax, sub, big), axis=0, keepdims=True)
    le = lt[(N_EXPERT_GROUPS - 1) * SUBLANES:N_EXPERT_GROUPS * SUBLANES]
    for g in range(N_EXPERT_GROUPS - 2, -1, -1):
        le = jnp.where(gsel == g, lt[g * SUBLANES:(g + 1) * SUBLANES], le)
    m1 = jnp.max(le, axis=0, keepdims=True)
    pe = jnp.exp(le - m1)
    i1 = jnp.min(jnp.where(le == m1, sub, big), axis=0, keepdims=True)
    pe2 = jnp.where(sub == i1, -1.0, pe)
    p2 = jnp.max(pe2, axis=0, keepdims=True)
    i2 = jnp.min(jnp.where(pe2 == p2, sub, big), axis=0, keepdims=True)
    inv = 1.0 / (1.0 + p2)
    return (gsel * EXPERTS_PER_GROUP + i1, gsel * EXPERTS_PER_GROUP + i2,
            g1 * inv, g1 * p2 * inv)


def _derived_zero(x):
    bits = pltpu.bitcast(x, jnp.int32)
    return lax.shift_right_logical(lax.shift_right_logical(bits, 16), 16).astype(jnp.float32)


def _copy_chunks(src_ref, src_chunk, dst_ref, dst_chunk, sem, n=1):
    return pltpu.make_async_copy(src_ref.at[pl.ds(src_chunk, n)],
                                 dst_ref.at[pl.ds(dst_chunk, n)], sem)


def _drain(n_chunks, src_ref, dst_ref, sem):
    def group_body(k, c):
        _copy_chunks(src_ref, 0, dst_ref, 0, sem, WAIT_GROUP).wait()
        return c
    lax.fori_loop(0, lax.shift_right_logical(n_chunks, WAIT_GROUP_LOG2), group_body, 0)

    def one_body(k, c):
        _copy_chunks(src_ref, 0, dst_ref, 0, sem).wait()
        return c
    lax.fori_loop(0, n_chunks & (WAIT_GROUP - 1), one_body, 0)


def _issue_head(head, src_ref, src0, dst_ref, dst0, sem):
    pairs = lax.shift_right_logical(head, 1)
    for k in range(UNROLLED_CHUNKS // 2):
        @pl.when(k < pairs)
        def _():
            _copy_chunks(src_ref, src0 + 2 * k, dst_ref, dst0 + 2 * k, sem, 2).start()

    @pl.when((head & 1) == 1)
    def _():
        _copy_chunks(src_ref, src0 + 2 * pairs, dst_ref, dst0 + 2 * pairs, sem).start()


def _issue_rest(first, nch, src_ref, src0, dst_ref, dst0, sem):
    def body(k, c):
        _copy_chunks(src_ref, src0 + k, dst_ref, dst0 + k, sem).start()
        return c
    lax.fori_loop(first, nch, body, 0)


def _mixer_kernel(x_ref, xn_ref, meta_ref, nmix_ref, win_ref, convw_ref, nconv_ref, poolw_ref,
                  pscale_ref, npool_ref, wout_ref, nffn_ref, wr_ref,
                  h1_ref, route_ref, xs_ref, tab_ref, wexp_ref, wblk_ref, wnxt_ref, wsub_ref, nwork_ref,
                  extu_ref, extp_ref, hn_ref, h1s_ref, perm_ref, stage_ref, zero_ref, info_v_ref,
                  info_s_ref, run_ref, pend_ref, sem_info, sem_out,
                  *, tm, d_conv, cap_chunks, tiles_per_seq, max_work):
    s = pl.program_id(0)
    n_tiles = pl.num_programs(0) - 1
    cur = s % 2
    prv = 1 - cur
    st_sort = (s + 2) % 3
    st_disp = (s + 1) % 3
    nmix = nmix_ref[...]

    def info_copy(slot):
        return pltpu.make_async_copy(info_v_ref, info_s_ref.at[slot], sem_info)

    @pl.when(s == 0)
    def _():
        pend_ref[0] = 0
        zero_ref[...] = jnp.zeros_like(zero_ref)
        stage_ref[...] = jnp.zeros_like(stage_ref)
        h1s_ref[...] = jnp.zeros_like(h1s_ref)
        hn_ref[1] = _rms(x_ref[...], nmix).astype(jnp.bfloat16)
        info_v_ref[...] = jnp.zeros_like(info_v_ref)
        info_copy(0).start()
        info_copy(0).wait()
        info_copy(1).start()

        def init(e, c):
            run_ref[e] = 0
            return c
        lax.fori_loop(0, N_EXPERTS, init, 0)

    @pl.when(s % tiles_per_seq == 0)
    def _():
        hm = _rms(meta_ref[...], nmix).astype(jnp.bfloat16)
        pm = jnp.dot(hm, win_ref[:, d_conv:], preferred_element_type=jnp.float32)
        extu_ref[0:HALO, :] = pm[:, 0:d_conv] * pm[:, d_conv:2 * d_conv]
        extp_ref[0:HALO, :] = pm[:, 2 * d_conv:]

    info_copy(cur).wait()

    hn_ref[0] = hn_ref[1]
    hn = hn_ref[0]
    proj_bc = jnp.dot(hn, win_ref[:, 0:2 * d_conv], preferred_element_type=jnp.float32)

    hfp = _rms(h1s_ref[...], nffn_ref[...]).astype(jnp.bfloat16)
    lt = jnp.dot(hfp, wr_ref[...], preferred_element_type=jnp.float32).T
    e0, e1, gate0, gate1 = _route(lt)
    esub = lax.broadcasted_iota(jnp.int32, (N_EXPERTS, tm), 0)
    dead = (s == 0).astype(jnp.int32)
    oh0 = esub == e0 - dead * N_EXPERTS
    oh1 = esub == e1 - dead * N_EXPERTS
    oh = jnp.where(oh0, 1.0, jnp.where(oh1, 1.0, 0.0))
    tok = lax.broadcasted_iota(jnp.int32, (N_EXPERTS, tm), 1)
    ct = oh
    span = 1
    while span < tm:
        ct = ct + jnp.where(tok >= span, pltpu.roll(ct, span, 1), 0.0)
        span *= 2
    ct = ct - oh
    n_col = jnp.sum(oh, axis=1, keepdims=True)
    nch_col = jnp.floor((n_col + (CHUNK - 1)) * (1.0 / CHUNK))
    e_sub = lax.broadcasted_iota(jnp.int32, (N_EXPERTS, LANES), 0)
    l_lane = lax.broadcasted_iota(jnp.int32, (N_EXPERTS, LANES), 1)
    nch_row = jnp.sum(jnp.where(e_sub == l_lane, nch_col, 0.0), axis=0, keepdims=True)
    coff_col = jnp.sum(jnp.where(l_lane < e_sub, nch_row, 0.0), axis=1, keepdims=True)
    coff_row = jnp.sum(jnp.where(e_sub == l_lane, coff_col, 0.0), axis=0, keepdims=True)
    info_v_ref[...] = jnp.concatenate(
        [nch_row, coff_row, jnp.zeros((SUBLANES - 2, LANES), jnp.float32)], axis=0
    ).astype(jnp.int32)
    info_copy(prv).start()

    posbase = ct + coff_col * CHUNK
    pos0 = jnp.sum(jnp.where(oh0, posbase, 0.0), axis=0, keepdims=True)
    pos1 = jnp.sum(jnp.where(oh1, posbase, 0.0), axis=0, keepdims=True)
    route_t = jnp.concatenate(
        [pos0, pos1, gate0, gate1, jnp.zeros((LANES - 4, tm), jnp.float32)], axis=0)
    route_ref[...] = route_t.T

    pos0i = pos0.astype(jnp.int32)
    pos1i = pos1.astype(jnp.int32)
    for r0 in range(0, STAGE_ROWS, PERM_ROWS):
        r = lax.broadcasted_iota(jnp.int32, (PERM_ROWS, tm), 0) + r0
        perm_ref[r0:r0 + PERM_ROWS, :] = jnp.where(
            r == pos0i, 1.0, jnp.where(r == pos1i, 1.0, 0.0)).astype(jnp.bfloat16)

    x = x_ref[...]
    proj_vv = jnp.dot(hn, win_ref[:, 2 * d_conv:], preferred_element_type=jnp.float32)
    b_gate = proj_bc[:, 0:d_conv]
    v_pool = proj_vv[:, d_conv:]
    extu_ref[HALO:, :] = proj_bc[:, d_conv:] * proj_vv[:, 0:d_conv]
    extp_ref[HALO:, :] = v_pool

    hn_next = _rms(xn_ref[...], nmix).astype(jnp.bfloat16)
    anchor = proj_vv[tm - 1:tm, :] + hn_next[tm - 1:tm, :].astype(jnp.float32)
    hf_late = hfp + _derived_zero(anchor).astype(jnp.bfloat16)
    for r0 in range(0, STAGE_ROWS, PERM_ROWS):
        rows = jnp.dot(perm_ref[r0:r0 + PERM_ROWS, :], hf_late, preferred_element_type=jnp.float32)
        stage_ref[st_sort, r0 // CHUNK:(r0 + PERM_ROWS) // CHUNK] = (
            rows.astype(jnp.bfloat16).reshape(PERM_ROWS // CHUNK, CHUNK, x_ref.shape[-1]))

    conv = convw_ref[CONV_WIDTH - 1:CONV_WIDTH, :] * extu_ref[HALO:, :]
    for k in range(CONV_WIDTH - 1):
        off = HALO - (CONV_WIDTH - 1 - k)
        conv = conv + convw_ref[k:k + 1, :] * extu_ref[off:off + tm, :]
    y_conv = _rms(b_gate * conv, nconv_ref[...])

    mixed = []
    for g, w in enumerate(POOL_WINDOWS):
        cols = slice(g * LANES, (g + 1) * LANES)
        acc = extp_ref[:, cols]
        span = 1
        while span < w:
            acc = acc + pltpu.roll(acc, span, 0)
            span *= 2
        pooled = acc[HALO:] * (1.0 / w) - v_pool[:, cols]
        mixed.append(jnp.dot(pooled.astype(jnp.bfloat16), poolw_ref[g],
                             preferred_element_type=jnp.float32))
    y_pool = _rms(jnp.concatenate(mixed, axis=-1) * pscale_ref[...], npool_ref[...])

    y = jnp.concatenate([y_conv, y_pool], axis=-1).astype(jnp.bfloat16)
    h1 = x + jnp.dot(y, wout_ref[...], preferred_element_type=jnp.float32)
    h1_ref[...] = h1
    h1s_ref[...] = h1
    hn_ref[1] = hn_next

    extu_ref[0:HALO, :] = extu_ref[tm:tm + HALO, :]
    extp_ref[0:HALO, :] = extp_ref[tm:tm + HALO, :]

    pending = pend_ref[0]
    tile2 = jnp.maximum(s - 2, 0)
    tot = 0
    issued = 0
    most = 0
    plan = []
    for e in range(N_EXPERTS):
        nch = info_s_ref[cur, 0, e]
        base = run_ref[e]
        tab_ref[tile2 * TAB_STRIDE + e] = base
        tab_ref[tile2 * TAB_STRIDE + N_EXPERTS + e] = nch
        run_ref[e] = base + nch
        plan.append((nch > 0, info_s_ref[cur, 1, e], e * cap_chunks + base))
        tot = tot + nch
        issued = issued + jnp.where(nch > 0, SEGMENT_CHUNKS, 0) + jnp.maximum(nch - SEGMENT_CHUNKS, 0)
        most = jnp.maximum(most, nch)
    tab_ref[tile2 * TAB_STRIDE + 2 * N_EXPERTS] = tot
    pend_ref[0] = issued
    _drain(pending, zero_ref, xs_ref, sem_out)
    for nonempty, src0, dst0 in plan:
        @pl.when(nonempty)
        def _():
            _copy_chunks(stage_ref.at[st_disp], src0, xs_ref, dst0, sem_out, SEGMENT_CHUNKS).start()

    @pl.when(most > SEGMENT_CHUNKS)
    def _():
        def body(e, c):
            _issue_rest(SEGMENT_CHUNKS, info_s_ref[cur, 0, e],
                        stage_ref.at[st_disp], info_s_ref[cur, 1, e],
                        xs_ref, e * cap_chunks + tab_ref[tile2 * TAB_STRIDE + e], sem_out)
            return c
        lax.fori_loop(0, N_EXPERTS, body, 0)

    @pl.when(s == n_tiles)
    def _():
        info_copy(prv).wait()
        _drain(pend_ref[0], zero_ref, xs_ref, sem_out)
        sem_last = sem_out

        def last_body(e, tot):
            nch = info_s_ref[prv, 0, e]
            base = run_ref[e]
            tab_ref[(s - 1) * TAB_STRIDE + e] = base
            tab_ref[(s - 1) * TAB_STRIDE + N_EXPERTS + e] = nch
            _issue_rest(0, nch, stage_ref.at[st_sort], info_s_ref[prv, 1, e],
                        xs_ref, e * cap_chunks + base, sem_last)
            run_ref[e] = base + nch
            return tot + nch
        last = lax.fori_loop(0, N_EXPERTS, last_body, 0)
        tab_ref[(s - 1) * TAB_STRIDE + 2 * N_EXPERTS] = last
        _drain(last, zero_ref, xs_ref, sem_last)

        subs_per_tile = MOE_TILE // MOE_SUB

        def pad_body(e, carry):
            tot, w = carry
            n = run_ref[e]
            subs = (n + (MOE_SUB_CHUNKS - 1)) // MOE_SUB_CHUNKS
            tiles = (subs + (subs_per_tile - 1)) // subs_per_tile
            nch = subs * MOE_SUB_CHUNKS - n

            def chunk_body(k, c):
                _copy_chunks(zero_ref, 0, xs_ref, e * cap_chunks + n + k, sem_last).start()
                return c
            lax.fori_loop(0, nch, chunk_body, 0)

            def work_body(j, c):
                wexp_ref[w + j] = e
                wblk_ref[w + j] = e * (cap_chunks // MOE_CHUNKS) + j
                wsub_ref[w + j] = jnp.minimum(subs - j * subs_per_tile, subs_per_tile)
                return c
            lax.fori_loop(0, tiles, work_body, 0)
            return tot + nch, w + tiles
        n_pad, n_work = lax.fori_loop(0, N_EXPERTS, pad_body, (0, 0))
        _drain(n_pad, zero_ref, xs_ref, sem_last)
        nwork_ref[0] = n_work

        def next_body(k, nxt):
            w = n_work - 1 - k
            wnxt_ref[w] = nxt
            prev = wexp_ref[jnp.maximum(w - 1, 0)]
            return jnp.where(prev != wexp_ref[w], wexp_ref[w], nxt)
        lax.fori_loop(0, n_work, next_body, wexp_ref[n_work - 1])

        def fill_body(w, c):
            wexp_ref[w] = 0
            wblk_ref[w] = 0
            wnxt_ref[w] = 0
            wsub_ref[w] = 0
            return c
        lax.fori_loop(n_work, max_work, fill_body, 0)


def _moe_kernel(exp_ref, blk_ref, nxt_ref, sub_ref, n_ref, xs_ref, wg_ref, wu_ref, wd_ref, ys_ref,
                xbuf_ref, ybuf_ref, wgf_ref, wuf_ref, wdf_ref, wgu_bf_ref, wd_bf_ref,
                sem_x, sem_y, sem_w, *, d_exp, tile):
    n = n_ref[0]

    def x_copy(w, slot):
        return pltpu.make_async_copy(xs_ref.at[pl.ds(blk_ref[w] * tile, tile)],
                                     xbuf_ref.at[slot], sem_x.at[slot])

    def y_copy(w, slot):
        return pltpu.make_async_copy(ybuf_ref.at[slot],
                                     ys_ref.at[pl.ds(blk_ref[w] * tile, tile)], sem_y.at[slot])

    def w_copies(e, slot):
        return (pltpu.make_async_copy(wg_ref.at[e], wgf_ref.at[slot], sem_w.at[slot]),
                pltpu.make_async_copy(wu_ref.at[e], wuf_ref.at[slot], sem_w.at[slot]),
                pltpu.make_async_copy(wd_ref.at[e], wdf_ref.at[slot], sem_w.at[slot]))

    for c in w_copies(exp_ref[0], 0):
        c.start()
    x_copy(0, 0).start()

    @pl.when(n > 1)
    def _():
        x_copy(1, 1).start()

    def item(w, run):
        e = exp_ref[w]
        first = (w == 0) | (exp_ref[jnp.maximum(w - 1, 0)] != e)
        wslot = run % 2
        xslot = w % 3
        yslot = w % 2

        @pl.when(first)
        def _():
            for c in w_copies(e, wslot):
                c.wait()
            wgu_bf_ref[:, :d_exp] = wgf_ref[wslot].astype(jnp.bfloat16)
            wgu_bf_ref[:, d_exp:] = wuf_ref[wslot].astype(jnp.bfloat16)
            wd_bf_ref[...] = wdf_ref[wslot].astype(jnp.bfloat16)

            @pl.when(nxt_ref[w] != e)
            def _():
                for c in w_copies(nxt_ref[w], 1 - wslot):
                    c.start()

        x_copy(w, xslot).wait()

        @pl.when(w + 2 < n)
        def _():
            x_copy(w + 2, (w + 2) % 3).start()

        @pl.when(w >= 2)
        def _():
            y_copy(w - 2, yslot).wait()

        for k in range(1, tile // MOE_SUB + 1):
            @pl.when(sub_ref[w] == k)
            def _():
                rows = k * MOE_SUB
                gu = jnp.dot(xbuf_ref[xslot, 0:rows], wgu_bf_ref[...],
                             preferred_element_type=jnp.float32)
                g = gu[:, :d_exp]
                a = g * jax.nn.sigmoid(g) * gu[:, d_exp:]
                ybuf_ref[yslot, 0:rows] = jnp.dot(
                    a.astype(jnp.bfloat16), wd_bf_ref[...],
                    preferred_element_type=jnp.float32).astype(jnp.bfloat16)
        y_copy(w, yslot).start()
        return run + first.astype(jnp.int32)
    lax.fori_loop(0, n, item, 0)

    @pl.when(n >= 2)
    def _():
        y_copy(n - 2, n % 2).wait()
    y_copy(n - 1, (n - 1) % 2).wait()


def _combine_kernel(tab_ref, h1_ref, route_ref, fn_ref, ys_ref, out_ref, stage_ref, sem,
                    *, tm, cap_chunks):
    s = pl.program_id(0)
    n_steps = pl.num_programs(0)
    slot = s % 3
    ahead = (s + 2) % 3

    @pl.when(s == 0)
    def _():
        stage_ref[...] = jnp.zeros_like(stage_ref)
        for j in range(2):
            def body(e, coff):
                nch = tab_ref[j * TAB_STRIDE + N_EXPERTS + e]
                _issue_rest(0, nch, ys_ref, e * cap_chunks + tab_ref[j * TAB_STRIDE + e],
                            stage_ref.at[j], coff, sem.at[j])
                return coff + nch
            lax.fori_loop(0, N_EXPERTS, body, 0)

    _drain(tab_ref[s * TAB_STRIDE + 2 * N_EXPERTS], ys_ref, stage_ref.at[slot], sem.at[slot])

    route = route_ref[...]
    pos0 = route[:, 0:1].astype(jnp.int32)
    pos1 = route[:, 1:2].astype(jnp.int32)
    gate0 = route[:, 2:3]
    gate1 = route[:, 3:4]
    n_chunks = tab_ref[s * TAB_STRIDE + 2 * N_EXPERTS]

    def combine(n_rows):
        r = lax.broadcasted_iota(jnp.int32, (tm, n_rows), 1)
        pw = jnp.where(r == pos0, gate0, jnp.where(r == pos1, gate1, 0.0)).astype(jnp.bfloat16)
        rows = stage_ref[slot, 0:n_rows // CHUNK].reshape(n_rows, out_ref.shape[-1])
        moe = jnp.dot(pw, rows, preferred_element_type=jnp.float32)
        out_ref[...] = _rms(h1_ref[...] + moe, fn_ref[...])

    @pl.when(n_chunks <= COMMON_ROWS // CHUNK)
    def _():
        combine(COMMON_ROWS)

    @pl.when(n_chunks > COMMON_ROWS // CHUNK)
    def _():
        combine(STAGE_ROWS)

    nxt = jnp.minimum(s + 2, n_steps - 1)
    has_next = (s + 2 < n_steps).astype(jnp.int32)
    coff = 0
    most = 0
    plan = []
    for e in range(N_EXPERTS):
        nch = tab_ref[nxt * TAB_STRIDE + N_EXPERTS + e] * has_next
        plan.append((jnp.minimum(nch, UNROLLED_CHUNKS),
                     e * cap_chunks + tab_ref[nxt * TAB_STRIDE + e], coff))
        coff = coff + nch
        most = jnp.maximum(most, nch)
    for head, src0, dst0 in plan:
        _issue_head(head, ys_ref, src0, stage_ref.at[ahead], dst0, sem.at[ahead])

    @pl.when(most > UNROLLED_CHUNKS)
    def _():
        def body(e, coff):
            nch = tab_ref[nxt * TAB_STRIDE + N_EXPERTS + e]
            _issue_rest(UNROLLED_CHUNKS, nch, ys_ref, e * cap_chunks + tab_ref[nxt * TAB_STRIDE + e],
                        stage_ref.at[ahead], coff, sem.at[ahead])
            return coff + nch
        lax.fori_loop(0, N_EXPERTS, body, 0)


def kernel(x, meta_tokens, norm_mix, w_in, conv_w, norm_conv_out, pool_w, pool_scale,
           norm_pool_out, w_out, norm_ffn, w_router_group, w_router_expert,
           w_gate, w_up, w_down, final_norm):
    bsz, seq, d = x.shape
    d_conv = conv_w.shape[-1]
    d_exp = w_gate.shape[-1]
    n_exp = w_gate.shape[1]
    assert meta_tokens.shape[0] == N_META == HALO and n_exp == N_EXPERTS
    assert seq % MIX_TILE == 0
    bf = jnp.bfloat16
    i32 = jnp.int32
    t = bsz * seq
    n_tiles = t // MIX_TILE
    cap = -(-(t + CHUNK * n_tiles) // MOE_TILE) * MOE_TILE + MOE_TILE
    n_chunks = N_EXPERTS * cap // CHUNK
    max_work = (2 * t + N_EXPERTS * (CHUNK - 1) * n_tiles) // MOE_TILE + N_EXPERTS
    row = lambda a: a.reshape(1, -1)
    wr = jnp.concatenate(
        [w_router_expert[0], w_router_group[0],
         jnp.zeros((d, LANES - N_EXPERTS - N_EXPERT_GROUPS), jnp.float32)], axis=1).astype(bf)

    full = lambda shape: pl.BlockSpec(shape, lambda s: (0,) * len(shape))
    last = n_tiles - 1
    smem = pl.BlockSpec(memory_space=pltpu.SMEM)
    x2 = x.reshape(t, d)
    h1, route, xs, tab, wexp, wblk, wnxt, wsub, nwork = pl.pallas_call(
        functools.partial(_mixer_kernel, tm=MIX_TILE, d_conv=d_conv, cap_chunks=cap // CHUNK,
                          tiles_per_seq=seq // MIX_TILE, max_work=max_work),
        grid=(n_tiles + 1,),
        in_specs=[pl.BlockSpec((MIX_TILE, d), lambda s: (jnp.minimum(s, last), 0)),
                  pl.BlockSpec((MIX_TILE, d), lambda s: (jnp.minimum(s + 1, last), 0)),
                  full((N_META, d)), full((1, d)), full((d, 4 * d_conv)),
                  full((CONV_WIDTH, d_conv)), full((1, d_conv)),
                  full((len(POOL_WINDOWS), LANES, LANES)), full((1, d_conv)), full((1, d_conv)),
                  full((2 * d_conv, d)), full((1, d)), full((d, LANES))],
        out_specs=[pl.BlockSpec((MIX_TILE, d), lambda s: (s, 0)),
                   pl.BlockSpec((MIX_TILE, LANES), lambda s: (jnp.maximum(s - 1, 0), 0)),
                   pl.BlockSpec(memory_space=pl.ANY), smem, smem, smem, smem, smem, smem],
        out_shape=[jax.ShapeDtypeStruct((t + MIX_TILE, d), jnp.float32),
                   jax.ShapeDtypeStruct((t, LANES), jnp.float32),
                   jax.ShapeDtypeStruct((n_chunks, CHUNK, d), bf),
                   jax.ShapeDtypeStruct((n_tiles * TAB_STRIDE,), i32),
                   jax.ShapeDtypeStruct((max_work,), i32),
                   jax.ShapeDtypeStruct((max_work,), i32),
                   jax.ShapeDtypeStruct((max_work,), i32),
                   jax.ShapeDtypeStruct((max_work,), i32),
                   jax.ShapeDtypeStruct((1,), i32)],
        scratch_shapes=[pltpu.VMEM((MIX_TILE + HALO, d_conv), jnp.float32),
                        pltpu.VMEM((MIX_TILE + HALO, d_conv), jnp.float32),
                        pltpu.VMEM((2, MIX_TILE, d), bf),
                        pltpu.VMEM((MIX_TILE, d), jnp.float32),
                        pltpu.VMEM((STAGE_ROWS, MIX_TILE), bf),
                        pltpu.VMEM((3, STAGE_CHUNKS + SEGMENT_CHUNKS, CHUNK, d), bf),
                        pltpu.VMEM((WAIT_GROUP, CHUNK, d), bf),
                        pltpu.VMEM((SUBLANES, LANES), i32),
                        pltpu.SMEM((2, SUBLANES, LANES), i32),
                        pltpu.SMEM((N_EXPERTS,), i32),
                        pltpu.SMEM((1,), i32),
                        pltpu.SemaphoreType.DMA,
                        pltpu.SemaphoreType.DMA],
        compiler_params=pltpu.CompilerParams(
            dimension_semantics=("arbitrary",), vmem_limit_bytes=VMEM_LIMIT),
        name="mixer",
    )(x2, x2, meta_tokens, row(norm_mix), w_in[0].astype(bf), conv_w[0],
      row(norm_conv_out), pool_w[0].astype(bf), row(pool_scale), row(norm_pool_out),
      w_out[0].astype(bf), row(norm_ffn), wr)

    anyspec = pl.BlockSpec(memory_space=pl.ANY)
    f32 = jnp.float32
    ys = pl.pallas_call(
        functools.partial(_moe_kernel, d_exp=d_exp, tile=MOE_TILE),
        grid_spec=pltpu.PrefetchScalarGridSpec(
            num_scalar_prefetch=5,
            grid=(1,),
            in_specs=[anyspec, anyspec, anyspec, anyspec],
            out_specs=anyspec,
            scratch_shapes=[pltpu.VMEM((3, MOE_TILE, d), bf), pltpu.VMEM((2, MOE_TILE, d), bf),
                            pltpu.VMEM((2, d, d_exp), f32), pltpu.VMEM((2, d, d_exp), f32),
                            pltpu.VMEM((2, d_exp, d), f32),
                            pltpu.VMEM((d, 2 * d_exp), bf), pltpu.VMEM((d_exp, d), bf),
                            pltpu.SemaphoreType.DMA((3,)), pltpu.SemaphoreType.DMA((2,)),
                            pltpu.SemaphoreType.DMA((2,))]),
        out_shape=jax.ShapeDtypeStruct((n_chunks * CHUNK, d), bf),
        compiler_params=pltpu.CompilerParams(
            dimension_semantics=("arbitrary",), vmem_limit_bytes=VMEM_LIMIT),
        name="moe",
    )(wexp, wblk, wnxt, wsub, nwork, xs.reshape(n_chunks * CHUNK, d), w_gate[0], w_up[0], w_down[0])

    out = pl.pallas_call(
        functools.partial(_combine_kernel, tm=MIX_TILE, cap_chunks=cap // CHUNK),
        grid_spec=pltpu.PrefetchScalarGridSpec(
            num_scalar_prefetch=1,
            grid=(n_tiles,),
            in_specs=[pl.BlockSpec((MIX_TILE, d), lambda s, tb: (s, 0)),
                      pl.BlockSpec((MIX_TILE, LANES), lambda s, tb: (s, 0)),
                      pl.BlockSpec((1, d), lambda s, tb: (0, 0)),
                      pl.BlockSpec(memory_space=pl.ANY)],
            out_specs=pl.BlockSpec((MIX_TILE, d), lambda s, tb: (s, 0)),
            scratch_shapes=[pltpu.VMEM((3, STAGE_CHUNKS, CHUNK, d), bf),
                            pltpu.SemaphoreType.DMA((3,))]),
        out_shape=jax.ShapeDtypeStruct((t, d), jnp.float32),
        compiler_params=pltpu.CompilerParams(
            dimension_semantics=("arbitrary",), vmem_limit_bytes=VMEM_LIMIT),
        name="combine",
    )(tab, h1, route, row(final_norm), ys.reshape(n_chunks, CHUNK, d))
    return out.reshape(bsz, seq, d)
```

```python
import functools

import jax
import jax.numpy as jnp
from jax import lax
from jax.experimental import pallas as pl
from jax.experimental.pallas import tpu as pltpu

EPS = 1e-6
N_META = 16
CONV_WIDTH = 3
POOL_WINDOWS = (2, 4, 8, 16)
N_EXPERT_GROUPS = 4
EXPERTS_PER_GROUP = 8
N_EXPERTS = N_EXPERT_GROUPS * EXPERTS_PER_GROUP
LANES = 128
SUBLANES = 8
HALO = 16
CHUNK = 16

MIX_TILE = 512
MOE_TILE = 1024
MOE_SUB = 256
MOE_SUB_CHUNKS = MOE_SUB // CHUNK
MOE_CHUNKS = MOE_TILE // CHUNK
MOE_CHUNKS_LOG2 = MOE_CHUNKS.bit_length() - 1
VMEM_LIMIT = 56 * 1024 * 1024

STAGE_ROWS = 2 * MIX_TILE + N_EXPERTS * CHUNK
STAGE_CHUNKS = STAGE_ROWS // CHUNK
COMMON_ROWS = 2 * MIX_TILE + N_EXPERTS * CHUNK // 2
PERM_ROWS = 512
TAB_STRIDE = 2 * N_EXPERTS + 1
UNROLLED_CHUNKS = 5
SEGMENT_CHUNKS = 5
WAIT_GROUP = 16
WAIT_GROUP_LOG2 = WAIT_GROUP.bit_length() - 1


def _rms(x, g):
    return x * lax.rsqrt(jnp.mean(x * x, axis=-1, keepdims=True) + EPS) * g


def _route(lt):
    tm = lt.shape[1]
    sub = lax.broadcasted_iota(jnp.int32, (SUBLANES, tm), 0)
    neg = jnp.float32(-jnp.inf)
    big = jnp.int32(1 << 20)
    lg = jnp.where(sub < N_EXPERT_GROUPS, lt[N_EXPERTS:N_EXPERTS + SUBLANES], neg)
    gmax = jnp.max(lg, axis=0, keepdims=True)
    g1 = 1.0 / jnp.sum(jnp.exp(lg - gmax), axis=0, keepdims=True)
    gsel = jnp.min(jnp.where(lg == gmax, sub, big), axis=0, keepdims=True)
    le = lt[(N_EXPERT_GROUPS - 1) * SUBLANES:N_EXPERT_GROUPS * SUBLANES]
    for g in range(N_EXPERT_GROUPS - 2, -1, -1):
        le = jnp.where(gsel == g, lt[g * SUBLANES:(g + 1) * SUBLANES], le)
    m1 = jnp.max(le, axis=0, keepdims=True)
    pe = jnp.exp(le - m1)
    i1 = jnp.min(jnp.where(le == m1, sub, big), axis=0, keepdims=True)
    pe2 = jnp.where(sub == i1, -1.0, pe)
    p2 = jnp.max(pe2, axis=0, keepdims=True)
    i2 = jnp.min(jnp.where(pe2 == p2, sub, big), axis=0, keepdims=True)
    inv = 1.0 / (1.0 + p2)
    return (gsel * EXPERTS_PER_GROUP + i1, gsel * EXPERTS_PER_GROUP + i2,
            g1 * inv, g1 * p2 * inv)


def _derived_zero(x):
    bits = pltpu.bitcast(x, jnp.int32)
    return lax.shift_right_logical(lax.shift_right_logical(bits, 16), 16).astype(jnp.float32)


def _copy_chunks(src_ref, src_chunk, dst_ref, dst_chunk, sem, n=1):
    return pltpu.make_async_copy(src_ref.at[pl.ds(src_chunk, n)],
                                 dst_ref.at[pl.ds(dst_chunk, n)], sem)


def _drain(n_chunks, src_ref, dst_ref, sem):
    def group_body(k, c):
        _copy_chunks(src_ref, 0, dst_ref, 0, sem, WAIT_GROUP).wait()
        return c
    lax.fori_loop(0, lax.shift_right_logical(n_chunks, WAIT_GROUP_LOG2), group_body, 0)

    def one_body(k, c):
        _copy_chunks(src_ref, 0, dst_ref, 0, sem).wait()
        return c
    lax.fori_loop(0, n_chunks & (WAIT_GROUP - 1), one_body, 0)


def _issue_head(head, src_ref, src0, dst_ref, dst0, sem):
    pairs = lax.shift_right_logical(head, 1)
    for k in range(UNROLLED_CHUNKS // 2):
        @pl.when(k < pairs)
        def _():
            _copy_chunks(src_ref, src0 + 2 * k, dst_ref, dst0 + 2 * k, sem, 2).start()

    @pl.when((head & 1) == 1)
    def _():
        _copy_chunks(src_ref, src0 + 2 * pairs, dst_ref, dst0 + 2 * pairs, sem).start()


def _issue_rest(first, nch, src_ref, src0, dst_ref, dst0, sem):
    def body(k, c):
        _copy_chunks(src_ref, src0 + k, dst_ref, dst0 + k, sem).start()
        return c
    lax.fori_loop(first, nch, body, 0)


def _mixer_kernel(x_ref, xn_ref, meta_ref, nmix_ref, win_ref, convw_ref, nconv_ref, poolw_ref,
                  pscale_ref, npool_ref, wout_ref, nffn_ref, wr_ref,
                  h1_ref, route_ref, xs_ref, tab_ref, wexp_ref, wblk_ref, wnxt_ref, wsub_ref, nwork_ref,
                  extu_ref, extp_ref, hn_ref, h1s_ref, perm_ref, stage_ref, zero_ref, info_v_ref,
                  info_s_ref, run_ref, pend_ref, sem_info, sem_out,
                  *, tm, d_conv, cap_chunks, tiles_per_seq, max_work):
    s = pl.program_id(0)
    n_tiles = pl.num_programs(0) - 1
    cur = s % 2
    prv = 1 - cur
    st_sort = (s + 2) % 3
    st_disp = (s + 1) % 3
    nmix = nmix_ref[...]

    def info_copy(slot):
        return pltpu.make_async_copy(info_v_ref, info_s_ref.at[slot], sem_info)

    @pl.when(s == 0)
    def _():
        pend_ref[0] = 0
        zero_ref[...] = jnp.zeros_like(zero_ref)
        stage_ref[...] = jnp.zeros_like(stage_ref)
        h1s_ref[...] = jnp.zeros_like(h1s_ref)
        hn_ref[1] = _rms(x_ref[...], nmix).astype(jnp.bfloat16)
        info_v_ref[...] = jnp.zeros_like(info_v_ref)
        info_copy(0).start()
        info_copy(0).wait()
        info_copy(1).start()

        def init(e, c):
            run_ref[e] = 0
            return c
        lax.fori_loop(0, N_EXPERTS, init, 0)

    @pl.when(s % tiles_per_seq == 0)
    def _():
        hm = _rms(meta_ref[...], nmix).astype(jnp.bfloat16)
        pm = jnp.dot(hm, win_ref[:, d_conv:], preferred_element_type=jnp.float32)
        extu_ref[0:HALO, :] = pm[:, 0:d_conv] * pm[:, d_conv:2 * d_conv]
        extp_ref[0:HALO, :] = pm[:, 2 * d_conv:]

    info_copy(cur).wait()

    hn_ref[0] = hn_ref[1]
    hn = hn_ref[0]
    proj_bc = jnp.dot(hn, win_ref[:, 0:2 * d_conv], preferred_element_type=jnp.float32)

    hfp = _rms(h1s_ref[...], nffn_ref[...]).astype(jnp.bfloat16)
    lt = jnp.dot(hfp, wr_ref[...], preferred_element_type=jnp.float32).T
    e0, e1, gate0, gate1 = _route(lt)
    esub = lax.broadcasted_iota(jnp.int32, (N_EXPERTS, tm), 0)
    dead = (s == 0).astype(jnp.int32)
    oh0 = esub == e0 - dead * N_EXPERTS
    oh1 = esub == e1 - dead * N_EXPERTS
    oh = jnp.where(oh0, 1.0, jnp.where(oh1, 1.0, 0.0))
    tok = lax.broadcasted_iota(jnp.int32, (N_EXPERTS, tm), 1)
    ct = oh
    span = 1
    while span < tm:
        ct = ct + jnp.where(tok >= span, pltpu.roll(ct, span, 1), 0.0)
        span *= 2
    ct = ct - oh
    n_col = jnp.sum(oh, axis=1, keepdims=True)
    nch_col = jnp.floor((n_col + (CHUNK - 1)) * (1.0 / CHUNK))
    e_sub = lax.broadcasted_iota(jnp.int32, (N_EXPERTS, LANES), 0)
    l_lane = lax.broadcasted_iota(jnp.int32, (N_EXPERTS, LANES), 1)
    nch_row = jnp.sum(jnp.where(e_sub == l_lane, nch_col, 0.0), axis=0, keepdims=True)
    coff_col = jnp.sum(jnp.where(l_lane < e_sub, nch_row, 0.0), axis=1, keepdims=True)
    coff_row = jnp.sum(jnp.where(e_sub == l_lane, coff_col, 0.0), axis=0, keepdims=True)
    info_v_ref[...] = jnp.concatenate(
        [nch_row, coff_row, jnp.zeros((SUBLANES - 2, LANES), jnp.float32)], axis=0
    ).astype(jnp.int32)
    info_copy(prv).start()

    posbase = ct + coff_col * CHUNK
    pos0 = jnp.sum(jnp.where(oh0, posbase, 0.0), axis=0, keepdims=True)
    pos1 = jnp.sum(jnp.where(oh1, posbase, 0.0), axis=0, keepdims=True)
    route_t = jnp.concatenate(
        [pos0, pos1, gate0, gate1, jnp.zeros((LANES - 4, tm), jnp.float32)], axis=0)
    route_ref[...] = route_t.T

    pos0i = pos0.astype(jnp.int32)
    pos1i = pos1.astype(jnp.int32)
    for r0 in range(0, STAGE_ROWS, PERM_ROWS):
        r = lax.broadcasted_iota(jnp.int32, (PERM_ROWS, tm), 0) + r0
        perm_ref[r0:r0 + PERM_ROWS, :] = jnp.where(
            r == pos0i, 1.0, jnp.where(r == pos1i, 1.0, 0.0)).astype(jnp.bfloat16)

    x = x_ref[...]
    proj_vv = jnp.dot(hn, win_ref[:, 2 * d_conv:], preferred_element_type=jnp.float32)
    b_gate = proj_bc[:, 0:d_conv]
    v_pool = proj_vv[:, d_conv:]
    extu_ref[HALO:, :] = proj_bc[:, d_conv:] * proj_vv[:, 0:d_conv]
    extp_ref[HALO:, :] = v_pool

    hn_next = _rms(xn_ref[...], nmix).astype(jnp.bfloat16)
    anchor = proj_vv[tm - 1:tm, :] + hn_next[tm - 1:tm, :].astype(jnp.float32)
    hf_late = hfp + _derived_zero(anchor).astype(jnp.bfloat16)
    for r0 in range(0, STAGE_ROWS, PERM_ROWS):
        rows = jnp.dot(perm_ref[r0:r0 + PERM_ROWS, :], hf_late, preferred_element_type=jnp.float32)
        stage_ref[st_sort, r0 // CHUNK:(r0 + PERM_ROWS) // CHUNK] = (
            rows.astype(jnp.bfloat16).reshape(PERM_ROWS // CHUNK, CHUNK, x_ref.shape[-1]))

    conv = convw_ref[CONV_WIDTH - 1:CONV_WIDTH, :] * extu_ref[HALO:, :]
    for k in range(CONV_WIDTH - 1):
        off = HALO - (CONV_WIDTH - 1 - k)
        conv = conv + convw_ref[k:k + 1, :] * extu_ref[off:off + tm, :]
    y_conv = _rms(b_gate * conv, nconv_ref[...])

    mixed = []
    for g, w in enumerate(POOL_WINDOWS):
        cols = slice(g * LANES, (g + 1) * LANES)
        acc = extp_ref[:, cols]
        span = 1
        while span < w:
            acc = acc + pltpu.roll(acc, span, 0)
            span *= 2
        pooled = acc[HALO:] * (1.0 / w) - v_pool[:, cols]
        mixed.append(jnp.dot(pooled.astype(jnp.bfloat16), poolw_ref[g],
                             preferred_element_type=jnp.float32))
    y_pool = _rms(jnp.concatenate(mixed, axis=-1) * pscale_ref[...], npool_ref[...])

    y = jnp.concatenate([y_conv, y_pool], axis=-1).astype(jnp.bfloat16)
    h1 = x + jnp.dot(y, wout_ref[...], preferred_element_type=jnp.float32)
    h1_ref[...] = h1
    h1s_ref[...] = h1
    hn_ref[1] = hn_next

    extu_ref[0:HALO, :] = extu_ref[tm:tm + HALO, :]
    extp_ref[0:HALO, :] = extp_ref[tm:tm + HALO, :]

    pending = pend_ref[0]
    tile2 = jnp.maximum(s - 2, 0)
    tot = 0
    issued = 0
    most = 0
    plan = []
    for e in range(N_EXPERTS):
        nch = info_s_ref[cur, 0, e]
        base = run_ref[e]
        tab_ref[tile2 * TAB_STRIDE + e] = base
        tab_ref[tile2 * TAB_STRIDE + N_EXPERTS + e] = nch
        run_ref[e] = base + nch
        plan.append((nch > 0, info_s_ref[cur, 1, e], e * cap_chunks + base))
        tot = tot + nch
        issued = issued + jnp.where(nch > 0, SEGMENT_CHUNKS, 0) + jnp.maximum(nch - SEGMENT_CHUNKS, 0)
        most = jnp.maximum(most, nch)
    tab_ref[tile2 * TAB_STRIDE + 2 * N_EXPERTS] = tot
    pend_ref[0] = issued
    _drain(pending, zero_ref, xs_ref, sem_out)
    for nonempty, src0, dst0 in plan:
        @pl.when(nonempty)
        def _():
            _copy_chunks(stage_ref.at[st_disp], src0, xs_ref, dst0, sem_out, SEGMENT_CHUNKS).start()

    @pl.when(most > SEGMENT_CHUNKS)
    def _():
        def body(e, c):
            _issue_rest(SEGMENT_CHUNKS, info_s_ref[cur, 0, e],
                        stage_ref.at[st_disp], info_s_ref[cur, 1, e],
                        xs_ref, e * cap_chunks + tab_ref[tile2 * TAB_STRIDE + e], sem_out)
            return c
        lax.fori_loop(0, N_EXPERTS, body, 0)

    @pl.when(s == n_tiles)
    def _():
        info_copy(prv).wait()
        _drain(pend_ref[0], zero_ref, xs_ref, sem_out)
        sem_last = sem_out

        def last_body(e, tot):
            nch = info_s_ref[prv, 0, e]
            base = run_ref[e]
            tab_ref[(s - 1) * TAB_STRIDE + e] = base
            tab_ref[(s - 1) * TAB_STRIDE + N_EXPERTS + e] = nch
            _issue_rest(0, nch, stage_ref.at[st_sort], info_s_ref[prv, 1, e],
                        xs_ref, e * cap_chunks + base, sem_last)
            run_ref[e] = base + nch
            return tot + nch
        last = lax.fori_loop(0, N_EXPERTS, last_body, 0)
        tab_ref[(s - 1) * TAB_STRIDE + 2 * N_EXPERTS] = last
        _drain(last, zero_ref, xs_ref, sem_last)

        subs_per_tile = MOE_TILE // MOE_SUB

        def pad_body(e, carry):
            tot, w = carry
            n = run_ref[e]
            subs = (n + (MOE_SUB_CHUNKS - 1)) // MOE_SUB_CHUNKS
            tiles = (subs + (subs_per_tile - 1)) // subs_per_tile
            nch = subs * MOE_SUB_CHUNKS - n

            def chunk_body(k, c):
                _copy_chunks(zero_ref, 0, xs_ref, e * cap_chunks + n + k, sem_last).start()
                return c
            lax.fori_loop(0, nch, chunk_body, 0)

            def work_body(j, c):
                wexp_ref[w + j] = e
                wblk_ref[w + j] = e * (cap_chunks // MOE_CHUNKS) + j
                wsub_ref[w + j] = jnp.minimum(subs - j * subs_per_tile, subs_per_tile)
                return c
            lax.fori_loop(0, tiles, work_body, 0)
            return tot + nch, w + tiles
        n_pad, n_work = lax.fori_loop(0, N_EXPERTS, pad_body, (0, 0))
        _drain(n_pad, zero_ref, xs_ref, sem_last)
        nwork_ref[0] = n_work

        def next_body(k, nxt):
            w = n_work - 1 - k
            wnxt_ref[w] = nxt
            prev = wexp_ref[jnp.maximum(w - 1, 0)]
            return jnp.where(prev != wexp_ref[w], wexp_ref[w], nxt)
        lax.fori_loop(0, n_work, next_body, wexp_ref[n_work - 1])

        def fill_body(w, c):
            wexp_ref[w] = 0
            wblk_ref[w] = 0
            wnxt_ref[w] = 0
            wsub_ref[w] = 0
            return c
        lax.fori_loop(n_work, max_work, fill_body, 0)


def _moe_kernel(exp_ref, blk_ref, nxt_ref, sub_ref, n_ref, xs_ref, wg_ref, wu_ref, wd_ref, ys_ref,
                xbuf_ref, ybuf_ref, wgf_ref, wuf_ref, wdf_ref, wgu_bf_ref, wd_bf_ref,
                sem_x, sem_y, sem_w, *, d_exp, tile):
    n = n_ref[0]

    def x_copy(w, slot):
        return pltpu.make_async_copy(xs_ref.at[pl.ds(blk_ref[w] * tile, tile)],
                                     xbuf_ref.at[slot], sem_x.at[slot])

    def y_copy(w, slot):
        return pltpu.make_async_copy(ybuf_ref.at[slot],
                                     ys_ref.at[pl.ds(blk_ref[w] * tile, tile)], sem_y.at[slot])

    def w_copies(e, slot):
        return (pltpu.make_async_copy(wg_ref.at[e], wgf_ref.at[slot], sem_w.at[slot]),
                pltpu.make_async_copy(wu_ref.at[e], wuf_ref.at[slot], sem_w.at[slot]),
                pltpu.make_async_copy(wd_ref.at[e], wdf_ref.at[slot], sem_w.at[slot]))

    for c in w_copies(exp_ref[0], 0):
        c.start()
    x_copy(0, 0).start()

    @pl.when(n > 1)
    def _():
        x_copy(1, 1).start()

    def item(w, run):
        e = exp_ref[w]
        first = (w == 0) | (exp_ref[jnp.maximum(w - 1, 0)] != e)
        wslot = run % 2
        xslot = w % 3
        yslot = w % 2

        @pl.when(first)
        def _():
            for c in w_copies(e, wslot):
                c.wait()
            wgu_bf_ref[:, :d_exp] = wgf_ref[wslot].astype(jnp.bfloat16)
            wgu_bf_ref[:, d_exp:] = wuf_ref[wslot].astype(jnp.bfloat16)
            wd_bf_ref[...] = wdf_ref[wslot].astype(jnp.bfloat16)

            @pl.when(nxt_ref[w] != e)
            def _():
                for c in w_copies(nxt_ref[w], 1 - wslot):
                    c.start()

        x_copy(w, xslot).wait()

        @pl.when(w + 2 < n)
        def _():
            x_copy(w + 2, (w + 2) % 3).start()

        @pl.when(w >= 2)
        def _():
            y_copy(w - 2, yslot).wait()

        for k in range(1, tile // MOE_SUB + 1):
            @pl.when(sub_ref[w] == k)
            def _():
                rows = k * MOE_SUB
                gu = jnp.dot(xbuf_ref[xslot, 0:rows], wgu_bf_ref[...],
                             preferred_element_type=jnp.float32)
                g = gu[:, :d_exp]
                a = g * jax.nn.sigmoid(g) * gu[:, d_exp:]
                ybuf_ref[yslot, 0:rows] = jnp.dot(
                    a.astype(jnp.bfloat16), wd_bf_ref[...],
                    preferred_element_type=jnp.float32).astype(jnp.bfloat16)
        y_copy(w, yslot).start()
        return run + first.astype(jnp.int32)
    lax.fori_loop(0, n, item, 0)

    @pl.when(n >= 2)
    def _():
        y_copy(n - 2, n % 2).wait()
    y_copy(n - 1, (n - 1) % 2).wait()


def _combine_kernel(tab_ref, h1_ref, route_ref, fn_ref, ys_ref, out_ref,
                    h1buf_ref, rbuf_ref, obuf_ref, stage_ref, sem_in, sem_out, sem,
                    *, tm, cap_chunks, n_tiles):
    d = obuf_ref.shape[-1]

    def in_copies(j, b):
        rows = pl.ds(j * tm, tm)
        return (pltpu.make_async_copy(h1_ref.at[rows], h1buf_ref.at[b], sem_in.at[b]),
                pltpu.make_async_copy(route_ref.at[rows], rbuf_ref.at[b], sem_in.at[b]))

    def out_copy(j, b):
        return pltpu.make_async_copy(obuf_ref.at[b], out_ref.at[pl.ds(j * tm, tm)], sem_out.at[b])

    stage_ref[...] = jnp.zeros_like(stage_ref)
    for j in range(2):
        def body(e, coff):
            nch = tab_ref[j * TAB_STRIDE + N_EXPERTS + e]
            _issue_rest(0, nch, ys_ref, e * cap_chunks + tab_ref[j * TAB_STRIDE + e],
                        stage_ref.at[j], coff, sem.at[j])
            return coff + nch
        lax.fori_loop(0, N_EXPERTS, body, 0)
        for c in in_copies(j, j):
            c.start()

    def tile_body(s, carry):
        slot = s % 3
        ahead = (s + 2) % 3
        b = s % 2
        n_chunks = tab_ref[s * TAB_STRIDE + 2 * N_EXPERTS]
        _drain(n_chunks, ys_ref, stage_ref.at[slot], sem.at[slot])
        for c in in_copies(s, b):
            c.wait()

        @pl.when(s >= 2)
        def _():
            out_copy(s - 2, b).wait()

        route = rbuf_ref[b]
        pos0 = route[:, 0:1].astype(jnp.int32)
        pos1 = route[:, 1:2].astype(jnp.int32)
        gate0 = route[:, 2:3]
        gate1 = route[:, 3:4]

        def combine(n_rows):
            r = lax.broadcasted_iota(jnp.int32, (tm, n_rows), 1)
            pw = jnp.where(r == pos0, gate0, jnp.where(r == pos1, gate1, 0.0)).astype(jnp.bfloat16)
            rows = stage_ref[slot, 0:n_rows // CHUNK].reshape(n_rows, d)
            moe = jnp.dot(pw, rows, preferred_element_type=jnp.float32)
            obuf_ref[b] = _rms(h1buf_ref[b] + moe, fn_ref[...])

        @pl.when(n_chunks <= COMMON_ROWS // CHUNK)
        def _():
            combine(COMMON_ROWS)

        @pl.when(n_chunks > COMMON_ROWS // CHUNK)
        def _():
            combine(STAGE_ROWS)

        out_copy(s, b).start()

        @pl.when(s + 2 < n_tiles)
        def _():
            for c in in_copies(s + 2, b):
                c.start()
        nxt = jnp.minimum(s + 2, n_tiles - 1)
        has_next = jnp.where(s + 2 < n_tiles, 1, 0)
        coff = 0
        most = 0
        plan = []
        for e in range(N_EXPERTS):
            nch = tab_ref[nxt * TAB_STRIDE + N_EXPERTS + e] * has_next
            plan.append((jnp.minimum(nch, UNROLLED_CHUNKS),
                         e * cap_chunks + tab_ref[nxt * TAB_STRIDE + e], coff))
            coff = coff + nch
            most = jnp.maximum(most, nch)
        for head, src0, dst0 in plan:
            _issue_head(head, ys_ref, src0, stage_ref.at[ahead], dst0, sem.at[ahead])

        @pl.when(most > UNROLLED_CHUNKS)
        def _():
            def body(e, coff):
                nch = tab_ref[nxt * TAB_STRIDE + N_EXPERTS + e]
                _issue_rest(UNROLLED_CHUNKS, nch, ys_ref,
                            e * cap_chunks + tab_ref[nxt * TAB_STRIDE + e],
                            stage_ref.at[ahead], coff, sem.at[ahead])
                return coff + nch
            lax.fori_loop(0, N_EXPERTS, body, 0)
        return carry
    lax.fori_loop(0, n_tiles, tile_body, 0)

    out_copy(n_tiles - 2, n_tiles % 2).wait()
    out_copy(n_tiles - 1, (n_tiles - 1) % 2).wait()


def kernel(x, meta_tokens, norm_mix, w_in, conv_w, norm_conv_out, pool_w, pool_scale,
           norm_pool_out, w_out, norm_ffn, w_router_group, w_router_expert,
           w_gate, w_up, w_down, final_norm):
    bsz, seq, d = x.shape
    d_conv = conv_w.shape[-1]
    d_exp = w_gate.shape[-1]
    n_exp = w_gate.shape[1]
    assert meta_tokens.shape[0] == N_META == HALO and n_exp == N_EXPERTS
    assert seq % MIX_TILE == 0
    bf = jnp.bfloat16
    i32 = jnp.int32
    t = bsz * seq
    n_tiles = t // MIX_TILE
    cap = -(-(t + CHUNK * n_tiles) // MOE_TILE) * MOE_TILE + MOE_TILE
    n_chunks = N_EXPERTS * cap // CHUNK
    max_work = (2 * t + N_EXPERTS * (CHUNK - 1) * n_tiles) // MOE_TILE + N_EXPERTS
    row = lambda a: a.reshape(1, -1)
    wr = jnp.concatenate(
        [w_router_expert[0], w_router_group[0],
         jnp.zeros((d, LANES - N_EXPERTS - N_EXPERT_GROUPS), jnp.float32)], axis=1).astype(bf)

    full = lambda shape: pl.BlockSpec(shape, lambda s: (0,) * len(shape))
    last = n_tiles - 1
    smem = pl.BlockSpec(memory_space=pltpu.SMEM)
    x2 = x.reshape(t, d)
    h1, route, xs, tab, wexp, wblk, wnxt, wsub, nwork = pl.pallas_call(
        functools.partial(_mixer_kernel, tm=MIX_TILE, d_conv=d_conv, cap_chunks=cap // CHUNK,
                          tiles_per_seq=seq // MIX_TILE, max_work=max_work),
        grid=(n_tiles + 1,),
        in_specs=[pl.BlockSpec((MIX_TILE, d), lambda s: (jnp.minimum(s, last), 0)),
                  pl.BlockSpec((MIX_TILE, d), lambda s: (jnp.minimum(s + 1, last), 0)),
                  full((N_META, d)), full((1, d)), full((d, 4 * d_conv)),
                  full((CONV_WIDTH, d_conv)), full((1, d_conv)),
                  full((len(POOL_WINDOWS), LANES, LANES)), full((1, d_conv)), full((1, d_conv)),
                  full((2 * d_conv, d)), full((1, d)), full((d, LANES))],
        out_specs=[pl.BlockSpec((MIX_TILE, d), lambda s: (s, 0)),
                   pl.BlockSpec((MIX_TILE, LANES), lambda s: (jnp.maximum(s - 1, 0), 0)),
                   pl.BlockSpec(memory_space=pl.ANY), smem, smem, smem, smem, smem, smem],
        out_shape=[jax.ShapeDtypeStruct((t + MIX_TILE, d), jnp.float32),
                   jax.ShapeDtypeStruct((t, LANES), jnp.float32),
                   jax.ShapeDtypeStruct((n_chunks, CHUNK, d), bf),
                   jax.ShapeDtypeStruct((n_tiles * TAB_STRIDE,), i32),
                   jax.ShapeDtypeStruct((max_work,), i32),
                   jax.ShapeDtypeStruct((max_work,), i32),
                   jax.ShapeDtypeStruct((max_work,), i32),
                   jax.ShapeDtypeStruct((max_work,), i32),
                   jax.ShapeDtypeStruct((1,), i32)],
        scratch_shapes=[pltpu.VMEM((MIX_TILE + HALO, d_conv), jnp.float32),
                        pltpu.VMEM((MIX_TILE + HALO, d_conv), jnp.float32),
                        pltpu.VMEM((2, MIX_TILE, d), bf),
                        pltpu.VMEM((MIX_TILE, d), jnp.float32),
                        pltpu.VMEM((STAGE_ROWS, MIX_TILE), bf),
                        pltpu.VMEM((3, STAGE_CHUNKS + SEGMENT_CHUNKS, CHUNK, d), bf),
                        pltpu.VMEM((WAIT_GROUP, CHUNK, d), bf),
                        pltpu.VMEM((SUBLANES, LANES), i32),
                        pltpu.SMEM((2, SUBLANES, LANES), i32),
                        pltpu.SMEM((N_EXPERTS,), i32),
                        pltpu.SMEM((1,), i32),
                        pltpu.SemaphoreType.DMA,
                        pltpu.SemaphoreType.DMA],
        compiler_params=pltpu.CompilerParams(
            dimension_semantics=("arbitrary",), vmem_limit_bytes=VMEM_LIMIT),
        name="mixer",
    )(x2, x2, meta_tokens, row(norm_mix), w_in[0].astype(bf), conv_w[0],
      row(norm_conv_out), pool_w[0].astype(bf), row(pool_scale), row(norm_pool_out),
      w_out[0].astype(bf), row(norm_ffn), wr)

    anyspec = pl.BlockSpec(memory_space=pl.ANY)
    f32 = jnp.float32
    ys = pl.pallas_call(
        functools.partial(_moe_kernel, d_exp=d_exp, tile=MOE_TILE),
        grid_spec=pltpu.PrefetchScalarGridSpec(
            num_scalar_prefetch=5,
            grid=(1,),
            in_specs=[anyspec, anyspec, anyspec, anyspec],
            out_specs=anyspec,
            scratch_shapes=[pltpu.VMEM((3, MOE_TILE, d), bf), pltpu.VMEM((2, MOE_TILE, d), bf),
                            pltpu.VMEM((2, d, d_exp), f32), pltpu.VMEM((2, d, d_exp), f32),
                            pltpu.VMEM((2, d_exp, d), f32),
                            pltpu.VMEM((d, 2 * d_exp), bf), pltpu.VMEM((d_exp, d), bf),
                            pltpu.SemaphoreType.DMA((3,)), pltpu.SemaphoreType.DMA((2,)),
                            pltpu.SemaphoreType.DMA((2,))]),
        out_shape=jax.ShapeDtypeStruct((n_chunks * CHUNK, d), bf),
        compiler_params=pltpu.CompilerParams(
            dimension_semantics=("arbitrary",), vmem_limit_bytes=VMEM_LIMIT),
        name="moe",
    )(wexp, wblk, wnxt, wsub, nwork, xs.reshape(n_chunks * CHUNK, d), w_gate[0], w_up[0], w_down[0])

    out = pl.pallas_call(
        functools.partial(_combine_kernel, tm=MIX_TILE, cap_chunks=cap // CHUNK, n_tiles=n_tiles),
        grid_spec=pltpu.PrefetchScalarGridSpec(
            num_scalar_prefetch=1,
            grid=(1,),
            in_specs=[anyspec, anyspec, pl.BlockSpec((1, d), lambda s, tb: (0, 0)), anyspec],
            out_specs=anyspec,
            scratch_shapes=[pltpu.VMEM((2, MIX_TILE, d), f32), pltpu.VMEM((2, MIX_TILE, LANES), f32),
                            pltpu.VMEM((2, MIX_TILE, d), f32),
                            pltpu.VMEM((3, STAGE_CHUNKS, CHUNK, d), bf),
                            pltpu.SemaphoreType.DMA((2,)), pltpu.SemaphoreType.DMA((2,)),
                            pltpu.SemaphoreType.DMA((3,))]),
        out_shape=jax.ShapeDtypeStruct((t, d), jnp.float32),
        compiler_params=pltpu.CompilerParams(
            dimension_semantics=("arbitrary",), vmem_limit_bytes=VMEM_LIMIT),
        name="combine",
    )(tab, h1, route, row(final_norm), ys.reshape(n_chunks, CHUNK, d))
    return out.reshape(bsz, seq, d)
```

```python
import functools

import jax
import jax.numpy as jnp
from jax import lax
from jax.experimental import pallas as pl
from jax.experimental.pallas import tpu as pltpu

EPS = 1e-6
N_META = 16
CONV_WIDTH = 3
POOL_WINDOWS = (2, 4, 8, 16)
N_EXPERT_GROUPS = 4
EXPERTS_PER_GROUP = 8
N_EXPERTS = N_EXPERT_GROUPS * EXPERTS_PER_GROUP
LANES = 128
SUBLANES = 8
HALO = 16
CHUNK = 16

MIX_TILE = 512
MOE_TILE = 1024
MOE_SUB = 256
MOE_SUB_CHUNKS = MOE_SUB // CHUNK
MOE_CHUNKS = MOE_TILE // CHUNK
MOE_CHUNKS_LOG2 = MOE_CHUNKS.bit_length() - 1
VMEM_LIMIT = 56 * 1024 * 1024

STAGE_ROWS = 2 * MIX_TILE + N_EXPERTS * CHUNK
STAGE_CHUNKS = STAGE_ROWS // CHUNK
COMMON_ROWS = 2 * MIX_TILE + N_EXPERTS * CHUNK // 2
PERM_ROWS = 512
TAB_STRIDE = 2 * N_EXPERTS + 1
UNROLLED_CHUNKS = 5
SEGMENT_CHUNKS = 5
WAIT_GROUP = 16
WAIT_GROUP_LOG2 = WAIT_GROUP.bit_length() - 1


def _rms(x, g):
    return x * lax.rsqrt(jnp.mean(x * x, axis=-1, keepdims=True) + EPS) * g


def _route(lt):
    tm = lt.shape[1]
    sub = lax.broadcasted_iota(jnp.int32, (SUBLANES, tm), 0)
    neg = jnp.float32(-jnp.inf)
    big = jnp.int32(1 << 20)
    lg = jnp.where(sub < N_EXPERT_GROUPS, lt[N_EXPERTS:N_EXPERTS + SUBLANES], neg)
    gmax = jnp.max(lg, axis=0, keepdims=True)
    g1 = 1.0 / jnp.sum(jnp.exp(lg - gmax), axis=0, keepdims=True)
    gsel = jnp.min(jnp.where(lg == gmax, sub, big), axis=0, keepdims=True)
    le = lt[(N_EXPERT_GROUPS - 1) * SUBLANES:N_EXPERT_GROUPS * SUBLANES]
    for g in range(N_EXPERT_GROUPS - 2, -1, -1):
        le = jnp.where(gsel == g, lt[g * SUBLANES:(g + 1) * SUBLANES], le)
    m1 = jnp.max(le, axis=0, keepdims=True)
    pe = jnp.exp(le - m1)
    i1 = jnp.min(jnp.where(le == m1, sub, big), axis=0, keepdims=True)
    pe2 = jnp.where(sub == i1, -1.0, pe)
    p2 = jnp.max(pe2, axis=0, keepdims=True)
    i2 = jnp.min(jnp.where(pe2 == p2, sub, big), axis=0, keepdims=True)
    inv = 1.0 / (1.0 + p2)
    return (gsel * EXPERTS_PER_GROUP + i1, gsel * EXPERTS_PER_GROUP + i2,
            g1 * inv, g1 * p2 * inv)


def _derived_zero(x):
    bits = pltpu.bitcast(x, jnp.int32)
    return lax.shift_right_logical(lax.shift_right_logical(bits, 16), 16).astype(jnp.float32)


def _copy_chunks(src_ref, src_chunk, dst_ref, dst_chunk, sem, n=1):
    return pltpu.make_async_copy(src_ref.at[pl.ds(src_chunk, n)],
                                 dst_ref.at[pl.ds(dst_chunk, n)], sem)


def _drain(n_chunks, src_ref, dst_ref, sem):
    def group_body(k, c):
        _copy_chunks(src_ref, 0, dst_ref, 0, sem, WAIT_GROUP).wait()
        return c
    lax.fori_loop(0, lax.shift_right_logical(n_chunks, WAIT_GROUP_LOG2), group_body, 0)

    def one_body(k, c):
        _copy_chunks(src_ref, 0, dst_ref, 0, sem).wait()
        return c
    lax.fori_loop(0, n_chunks & (WAIT_GROUP - 1), one_body, 0)


def _issue_head(head, src_ref, src0, dst_ref, dst0, sem):
    pairs = lax.shift_right_logical(head, 1)
    for k in range(UNROLLED_CHUNKS // 2):
        @pl.when(k < pairs)
        def _():
            _copy_chunks(src_ref, src0 + 2 * k, dst_ref, dst0 + 2 * k, sem, 2).start()

    @pl.when((head & 1) == 1)
    def _():
        _copy_chunks(src_ref, src0 + 2 * pairs, dst_ref, dst0 + 2 * pairs, sem).start()


def _issue_rest(first, nch, src_ref, src0, dst_ref, dst0, sem):
    def body(k, c):
        _copy_chunks(src_ref, src0 + k, dst_ref, dst0 + k, sem).start()
        return c
    lax.fori_loop(first, nch, body, 0)


def _mixer_kernel(x_ref, xn_ref, meta_ref, nmix_ref, win_ref, convw_ref, nconv_ref, poolw_ref,
                  pscale_ref, npool_ref, wout_ref, nffn_ref, wr_ref,
                  h1_ref, route_ref, xs_ref, tab_ref, wexp_ref, wblk_ref, wnxt_ref, wsub_ref, nwork_ref,
                  extu_ref, extp_ref, hn_ref, h1s_ref, perm_ref, stage_ref, zero_ref, info_v_ref,
                  info_s_ref, run_ref, pend_ref, sem_info, sem_out,
                  *, tm, d_conv, cap_chunks, tiles_per_seq, max_work):
    s = pl.program_id(0)
    n_tiles = pl.num_programs(0) - 1
    cur = s % 2
    prv = 1 - cur
    st_sort = (s + 2) % 3
    st_disp = (s + 1) % 3
    nmix = nmix_ref[...]

    def info_copy(slot):
        return pltpu.make_async_copy(info_v_ref, info_s_ref.at[slot], sem_info)

    @pl.when(s == 0)
    def _():
        pend_ref[0] = 0
        zero_ref[...] = jnp.zeros_like(zero_ref)
        stage_ref[...] = jnp.zeros_like(stage_ref)
        h1s_ref[...] = jnp.zeros_like(h1s_ref)
        hn_ref[1] = _rms(x_ref[...], nmix).astype(jnp.bfloat16)
        info_v_ref[...] = jnp.zeros_like(info_v_ref)
        info_copy(0).start()
        info_copy(0).wait()
        info_copy(1).start()

        def init(e, c):
            run_ref[e] = 0
            return c
        lax.fori_loop(0, N_EXPERTS, init, 0)

    @pl.when(s % tiles_per_seq == 0)
    def _():
        hm = _rms(meta_ref[...], nmix).astype(jnp.bfloat16)
        pm = jnp.dot(hm, win_ref[:, d_conv:], preferred_element_type=jnp.float32)
        extu_ref[0:HALO, :] = pm[:, 0:d_conv] * pm[:, d_conv:2 * d_conv]
        extp_ref[0:HALO, :] = pm[:, 2 * d_conv:]

    info_copy(cur).wait()

    hn_ref[0] = hn_ref[1]
    hn = hn_ref[0]
    proj_bc = jnp.dot(hn, win_ref[:, 0:2 * d_conv], preferred_element_type=jnp.float32)

    hfp = _rms(h1s_ref[...], nffn_ref[...]).astype(jnp.bfloat16)
    lt = jnp.dot(hfp, wr_ref[...], preferred_element_type=jnp.float32).T
    e0, e1, gate0, gate1 = _route(lt)
    esub = lax.broadcasted_iota(jnp.int32, (N_EXPERTS, tm), 0)
    dead = (s == 0).astype(jnp.int32)
    oh0 = esub == e0 - dead * N_EXPERTS
    oh1 = esub == e1 - dead * N_EXPERTS
    oh = jnp.where(oh0, 1.0, jnp.where(oh1, 1.0, 0.0))
    tok = lax.broadcasted_iota(jnp.int32, (N_EXPERTS, tm), 1)
    ct = oh
    span = 1
    while span < tm:
        ct = ct + jnp.where(tok >= span, pltpu.roll(ct, span, 1), 0.0)
        span *= 2
    ct = ct - oh
    n_col = jnp.sum(oh, axis=1, keepdims=True)
    nch_col = jnp.floor((n_col + (CHUNK - 1)) * (1.0 / CHUNK))
    e_sub = lax.broadcasted_iota(jnp.int32, (N_EXPERTS, LANES), 0)
    l_lane = lax.broadcasted_iota(jnp.int32, (N_EXPERTS, LANES), 1)
    nch_row = jnp.sum(jnp.where(e_sub == l_lane, nch_col, 0.0), axis=0, keepdims=True)
    coff_col = jnp.sum(jnp.where(l_lane < e_sub, nch_row, 0.0), axis=1, keepdims=True)
    coff_row = jnp.sum(jnp.where(e_sub == l_lane, coff_col, 0.0), axis=0, keepdims=True)
    info_v_ref[...] = jnp.concatenate(
        [nch_row, coff_row, jnp.zeros((SUBLANES - 2, LANES), jnp.float32)], axis=0
    ).astype(jnp.int32)
    info_copy(prv).start()

    posbase = ct + coff_col * CHUNK
    pos0 = jnp.sum(jnp.where(oh0, posbase, 0.0), axis=0, keepdims=True)
    pos1 = jnp.sum(jnp.where(oh1, posbase, 0.0), axis=0, keepdims=True)
    route_ref[...] = jnp.concatenate(
        [pos0, pos1, gate0, gate1, jnp.zeros((SUBLANES - 4, tm), jnp.float32)], axis=0)

    pos0i = pos0.astype(jnp.int32)
    pos1i = pos1.astype(jnp.int32)
    for r0 in range(0, STAGE_ROWS, PERM_ROWS):
        r = lax.broadcasted_iota(jnp.int32, (PERM_ROWS, tm), 0) + r0
        perm_ref[r0:r0 + PERM_ROWS, :] = jnp.where(
            r == pos0i, 1.0, jnp.where(r == pos1i, 1.0, 0.0)).astype(jnp.bfloat16)

    x = x_ref[...]
    proj_vv = jnp.dot(hn, win_ref[:, 2 * d_conv:], preferred_element_type=jnp.float32)
    b_gate = proj_bc[:, 0:d_conv]
    v_pool = proj_vv[:, d_conv:]
    extu_ref[HALO:, :] = proj_bc[:, d_conv:] * proj_vv[:, 0:d_conv]
    extp_ref[HALO:, :] = v_pool

    hn_next = _rms(xn_ref[...], nmix).astype(jnp.bfloat16)
    anchor = proj_vv[tm - 1:tm, :] + hn_next[tm - 1:tm, :].astype(jnp.float32)
    hf_late = hfp + _derived_zero(anchor).astype(jnp.bfloat16)
    for r0 in range(0, STAGE_ROWS, PERM_ROWS):
        rows = jnp.dot(perm_ref[r0:r0 + PERM_ROWS, :], hf_late, preferred_element_type=jnp.float32)
        stage_ref[st_sort, r0 // CHUNK:(r0 + PERM_ROWS) // CHUNK] = (
            rows.astype(jnp.bfloat16).reshape(PERM_ROWS // CHUNK, CHUNK, x_ref.shape[-1]))

    conv = convw_ref[CONV_WIDTH - 1:CONV_WIDTH, :] * extu_ref[HALO:, :]
    for k in range(CONV_WIDTH - 1):
        off = HALO - (CONV_WIDTH - 1 - k)
        conv = conv + convw_ref[k:k + 1, :] * extu_ref[off:off + tm, :]
    y_conv = _rms(b_gate * conv, nconv_ref[...])

    mixed = []
    for g, w in enumerate(POOL_WINDOWS):
        cols = slice(g * LANES, (g + 1) * LANES)
        acc = extp_ref[:, cols]
        span = 1
        while span < w:
            acc = acc + pltpu.roll(acc, span, 0)
            span *= 2
        pooled = acc[HALO:] * (1.0 / w) - v_pool[:, cols]
        mixed.append(jnp.dot(pooled.astype(jnp.bfloat16), poolw_ref[g],
                             preferred_element_type=jnp.float32))
    y_pool = _rms(jnp.concatenate(mixed, axis=-1) * pscale_ref[...], npool_ref[...])

    y = jnp.concatenate([y_conv, y_pool], axis=-1).astype(jnp.bfloat16)
    h1 = x + jnp.dot(y, wout_ref[...], preferred_element_type=jnp.float32)
    h1_ref[...] = h1
    h1s_ref[...] = h1
    hn_ref[1] = hn_next

    extu_ref[0:HALO, :] = extu_ref[tm:tm + HALO, :]
    extp_ref[0:HALO, :] = extp_ref[tm:tm + HALO, :]

    pending = pend_ref[0]
    tile2 = jnp.maximum(s - 2, 0)
    tot = 0
    issued = 0
    most = 0
    plan = []
    for e in range(N_EXPERTS):
        nch = info_s_ref[cur, 0, e]
        base = run_ref[e]
        tab_ref[tile2 * TAB_STRIDE + e] = base
        tab_ref[tile2 * TAB_STRIDE + N_EXPERTS + e] = nch
        run_ref[e] = base + nch
        plan.append((nch > 0, info_s_ref[cur, 1, e], e * cap_chunks + base))
        tot = tot + nch
        issued = issued + jnp.where(nch > 0, SEGMENT_CHUNKS, 0) + jnp.maximum(nch - SEGMENT_CHUNKS, 0)
        most = jnp.maximum(most, nch)
    tab_ref[tile2 * TAB_STRIDE + 2 * N_EXPERTS] = tot
    pend_ref[0] = issued
    _drain(pending, zero_ref, xs_ref, sem_out)
    for nonempty, src0, dst0 in plan:
        @pl.when(nonempty)
        def _():
            _copy_chunks(stage_ref.at[st_disp], src0, xs_ref, dst0, sem_out, SEGMENT_CHUNKS).start()

    @pl.when(most > SEGMENT_CHUNKS)
    def _():
        def body(e, c):
            _issue_rest(SEGMENT_CHUNKS, info_s_ref[cur, 0, e],
                        stage_ref.at[st_disp], info_s_ref[cur, 1, e],
                        xs_ref, e * cap_chunks + tab_ref[tile2 * TAB_STRIDE + e], sem_out)
            return c
        lax.fori_loop(0, N_EXPERTS, body, 0)

    @pl.when(s == n_tiles)
    def _():
        info_copy(prv).wait()
        _drain(pend_ref[0], zero_ref, xs_ref, sem_out)
        sem_last = sem_out

        def last_body(e, tot):
            nch = info_s_ref[prv, 0, e]
            base = run_ref[e]
            tab_ref[(s - 1) * TAB_STRIDE + e] = base
            tab_ref[(s - 1) * TAB_STRIDE + N_EXPERTS + e] = nch
            _issue_rest(0, nch, stage_ref.at[st_sort], info_s_ref[prv, 1, e],
                        xs_ref, e * cap_chunks + base, sem_last)
            run_ref[e] = base + nch
            return tot + nch
        last = lax.fori_loop(0, N_EXPERTS, last_body, 0)
        tab_ref[(s - 1) * TAB_STRIDE + 2 * N_EXPERTS] = last
        _drain(last, zero_ref, xs_ref, sem_last)

        subs_per_tile = MOE_TILE // MOE_SUB

        def pad_body(e, carry):
            tot, w = carry
            n = run_ref[e]
            subs = (n + (MOE_SUB_CHUNKS - 1)) // MOE_SUB_CHUNKS
            tiles = (subs + (subs_per_tile - 1)) // subs_per_tile
            nch = subs * MOE_SUB_CHUNKS - n

            def chunk_body(k, c):
                _copy_chunks(zero_ref, 0, xs_ref, e * cap_chunks + n + k, sem_last).start()
                return c
            lax.fori_loop(0, nch, chunk_body, 0)

            def work_body(j, c):
                wexp_ref[w + j] = e
                wblk_ref[w + j] = e * (cap_chunks // MOE_CHUNKS) + j
                wsub_ref[w + j] = jnp.minimum(subs - j * subs_per_tile, subs_per_tile)
                return c
            lax.fori_loop(0, tiles, work_body, 0)
            return tot + nch, w + tiles
        n_pad, n_work = lax.fori_loop(0, N_EXPERTS, pad_body, (0, 0))
        _drain(n_pad, zero_ref, xs_ref, sem_last)
        nwork_ref[0] = n_work

        def next_body(k, nxt):
            w = n_work - 1 - k
            wnxt_ref[w] = nxt
            prev = wexp_ref[jnp.maximum(w - 1, 0)]
            return jnp.where(prev != wexp_ref[w], wexp_ref[w], nxt)
        lax.fori_loop(0, n_work, next_body, wexp_ref[n_work - 1])

        def fill_body(w, c):
            wexp_ref[w] = 0
            wblk_ref[w] = 0
            wnxt_ref[w] = 0
            wsub_ref[w] = 0
            return c
        lax.fori_loop(n_work, max_work, fill_body, 0)


def _moe_kernel(exp_ref, blk_ref, nxt_ref, sub_ref, n_ref, xs_ref, wg_ref, wu_ref, wd_ref, ys_ref,
                xbuf_ref, ybuf_ref, wgf_ref, wuf_ref, wdf_ref, wgu_bf_ref, wd_bf_ref,
                sem_x, sem_y, sem_w, *, d_exp, tile):
    n = n_ref[0]
    sizes = range(1, tile // MOE_SUB + 1)

    class _Sized:
        def __init__(self, make, w, slot):
            self.make, self.w, self.slot = make, w, slot

        def _each(self, act):
            for k in sizes:
                @pl.when(sub_ref[self.w] == k)
                def _():
                    act(self.make(self.w, self.slot, k * MOE_SUB))

        def start(self):
            self._each(lambda c: c.start())

        def wait(self):
            self._each(lambda c: c.wait())

    def x_copy(w, slot):
        return _Sized(lambda w, slot, rows: pltpu.make_async_copy(
            xs_ref.at[pl.ds(blk_ref[w] * tile, rows)], xbuf_ref.at[slot, pl.ds(0, rows)],
            sem_x.at[slot]), w, slot)

    def y_copy(w, slot):
        return _Sized(lambda w, slot, rows: pltpu.make_async_copy(
            ybuf_ref.at[slot, pl.ds(0, rows)], ys_ref.at[pl.ds(blk_ref[w] * tile, rows)],
            sem_y.at[slot]), w, slot)

    def w_copies(e, slot):
        return (pltpu.make_async_copy(wg_ref.at[e], wgf_ref.at[slot], sem_w.at[slot]),
                pltpu.make_async_copy(wu_ref.at[e], wuf_ref.at[slot], sem_w.at[slot]),
                pltpu.make_async_copy(wd_ref.at[e], wdf_ref.at[slot], sem_w.at[slot]))

    for c in w_copies(exp_ref[0], 0):
        c.start()
    x_copy(0, 0).start()

    @pl.when(n > 1)
    def _():
        x_copy(1, 1).start()

    def item(w, run):
        e = exp_ref[w]
        first = (w == 0) | (exp_ref[jnp.maximum(w - 1, 0)] != e)
        wslot = run % 2
        xslot = w % 3
        yslot = w % 2

        @pl.when(first)
        def _():
            for c in w_copies(e, wslot):
                c.wait()
            wgu_bf_ref[:, :d_exp] = wgf_ref[wslot].astype(jnp.bfloat16)
            wgu_bf_ref[:, d_exp:] = wuf_ref[wslot].astype(jnp.bfloat16)
            wd_bf_ref[...] = wdf_ref[wslot].astype(jnp.bfloat16)

            @pl.when(nxt_ref[w] != e)
            def _():
                for c in w_copies(nxt_ref[w], 1 - wslot):
                    c.start()

        x_copy(w, xslot).wait()

        @pl.when(w + 2 < n)
        def _():
            x_copy(w + 2, (w + 2) % 3).start()

        @pl.when(w >= 2)
        def _():
            y_copy(w - 2, yslot).wait()

        for k in sizes:
            @pl.when(sub_ref[w] == k)
            def _():
                rows = k * MOE_SUB
                gu = jnp.dot(xbuf_ref[xslot, 0:rows], wgu_bf_ref[...],
                             preferred_element_type=jnp.float32)
                g = gu[:, :d_exp]
                a = g * jax.nn.sigmoid(g) * gu[:, d_exp:]
                ybuf_ref[yslot, 0:rows] = jnp.dot(
                    a.astype(jnp.bfloat16), wd_bf_ref[...],
                    preferred_element_type=jnp.float32).astype(jnp.bfloat16)
        y_copy(w, yslot).start()
        return run + first.astype(jnp.int32)
    lax.fori_loop(0, n, item, 0)

    @pl.when(n >= 2)
    def _():
        y_copy(n - 2, n % 2).wait()
    y_copy(n - 1, (n - 1) % 2).wait()


def _combine_kernel(tab_ref, h1_ref, route_ref, fn_ref, ys_ref, out_ref,
                    h1buf_ref, rbuf_ref, obuf_ref, stage_ref, sem_in, sem_out, sem,
                    *, tm, cap_chunks, n_tiles):
    d = obuf_ref.shape[-1]

    def in_copies(j, b):
        rows = pl.ds(j * tm, tm)
        return (pltpu.make_async_copy(h1_ref.at[rows], h1buf_ref.at[b], sem_in.at[b]),
                pltpu.make_async_copy(route_ref.at[:, rows], rbuf_ref.at[b], sem_in.at[b]))

    def out_copy(j, b):
        return pltpu.make_async_copy(obuf_ref.at[b], out_ref.at[pl.ds(j * tm, tm)], sem_out.at[b])

    stage_ref[...] = jnp.zeros_like(stage_ref)
    for j in range(2):
        def body(e, coff):
            nch = tab_ref[j * TAB_STRIDE + N_EXPERTS + e]
            _issue_rest(0, nch, ys_ref, e * cap_chunks + tab_ref[j * TAB_STRIDE + e],
                        stage_ref.at[j], coff, sem.at[j])
            return coff + nch
        lax.fori_loop(0, N_EXPERTS, body, 0)
        for c in in_copies(j, j):
            c.start()

    def tile_body(s, carry):
        slot = s % 3
        ahead = (s + 2) % 3
        b = s % 2
        n_chunks = tab_ref[s * TAB_STRIDE + 2 * N_EXPERTS]
        _drain(n_chunks, ys_ref, stage_ref.at[slot], sem.at[slot])
        for c in in_copies(s, b):
            c.wait()

        @pl.when(s >= 2)
        def _():
            out_copy(s - 2, b).wait()

        route = jnp.concatenate(
            [rbuf_ref[b], jnp.zeros((LANES - SUBLANES, tm), jnp.float32)], axis=0).T
        pos0 = route[:, 0:1].astype(jnp.int32)
        pos1 = route[:, 1:2].astype(jnp.int32)
        gate0 = route[:, 2:3]
        gate1 = route[:, 3:4]

        def combine(n_rows):
            r = lax.broadcasted_iota(jnp.int32, (tm, n_rows), 1)
            pw = jnp.where(r == pos0, gate0, jnp.where(r == pos1, gate1, 0.0)).astype(jnp.bfloat16)
            rows = stage_ref[slot, 0:n_rows // CHUNK].reshape(n_rows, d)
            moe = jnp.dot(pw, rows, preferred_element_type=jnp.float32)
            obuf_ref[b] = _rms(h1buf_ref[b] + moe, fn_ref[...])

        @pl.when(n_chunks <= COMMON_ROWS // CHUNK)
        def _():
            combine(COMMON_ROWS)

        @pl.when(n_chunks > COMMON_ROWS // CHUNK)
        def _():
            combine(STAGE_ROWS)

        out_copy(s, b).start()

        @pl.when(s + 2 < n_tiles)
        def _():
            for c in in_copies(s + 2, b):
                c.start()
        nxt = jnp.minimum(s + 2, n_tiles - 1)
        has_next = jnp.where(s + 2 < n_tiles, 1, 0)
        coff = 0
        most = 0
        plan = []
        for e in range(N_EXPERTS):
            nch = tab_ref[nxt * TAB_STRIDE + N_EXPERTS + e] * has_next
            plan.append((jnp.minimum(nch, UNROLLED_CHUNKS),
                         e * cap_chunks + tab_ref[nxt * TAB_STRIDE + e], coff))
            coff = coff + nch
            most = jnp.maximum(most, nch)
        for head, src0, dst0 in plan:
            _issue_head(head, ys_ref, src0, stage_ref.at[ahead], dst0, sem.at[ahead])

        @pl.when(most > UNROLLED_CHUNKS)
        def _():
            def body(e, coff):
                nch = tab_ref[nxt * TAB_STRIDE + N_EXPERTS + e]
                _issue_rest(UNROLLED_CHUNKS, nch, ys_ref,
                            e * cap_chunks + tab_ref[nxt * TAB_STRIDE + e],
                            stage_ref.at[ahead], coff, sem.at[ahead])
                return coff + nch
            lax.fori_loop(0, N_EXPERTS, body, 0)
        return carry
    lax.fori_loop(0, n_tiles, tile_body, 0)

    out_copy(n_tiles - 2, n_tiles % 2).wait()
    out_copy(n_tiles - 1, (n_tiles - 1) % 2).wait()


def kernel(x, meta_tokens, norm_mix, w_in, conv_w, norm_conv_out, pool_w, pool_scale,
           norm_pool_out, w_out, norm_ffn, w_router_group, w_router_expert,
           w_gate, w_up, w_down, final_norm):
    bsz, seq, d = x.shape
    d_conv = conv_w.shape[-1]
    d_exp = w_gate.shape[-1]
    n_exp = w_gate.shape[1]
    assert meta_tokens.shape[0] == N_META == HALO and n_exp == N_EXPERTS
    assert seq % MIX_TILE == 0
    bf = jnp.bfloat16
    i32 = jnp.int32
    t = bsz * seq
    n_tiles = t // MIX_TILE
    cap = -(-(t + CHUNK * n_tiles) // MOE_TILE) * MOE_TILE + MOE_TILE
    n_chunks = N_EXPERTS * cap // CHUNK
    max_work = (2 * t + N_EXPERTS * (CHUNK - 1) * n_tiles) // MOE_TILE + N_EXPERTS
    row = lambda a: a.reshape(1, -1)
    wr = jnp.concatenate(
        [w_router_expert[0], w_router_group[0],
         jnp.zeros((d, LANES - N_EXPERTS - N_EXPERT_GROUPS), jnp.float32)], axis=1).astype(bf)

    full = lambda shape: pl.BlockSpec(shape, lambda s: (0,) * len(shape))
    last = n_tiles - 1
    smem = pl.BlockSpec(memory_space=pltpu.SMEM)
    x2 = x.reshape(t, d)
    h1, route, xs, tab, wexp, wblk, wnxt, wsub, nwork = pl.pallas_call(
        functools.partial(_mixer_kernel, tm=MIX_TILE, d_conv=d_conv, cap_chunks=cap // CHUNK,
                          tiles_per_seq=seq // MIX_TILE, max_work=max_work),
        grid=(n_tiles + 1,),
        in_specs=[pl.BlockSpec((MIX_TILE, d), lambda s: (jnp.minimum(s, last), 0)),
                  pl.BlockSpec((MIX_TILE, d), lambda s: (jnp.minimum(s + 1, last), 0)),
                  full((N_META, d)), full((1, d)), full((d, 4 * d_conv)),
                  full((CONV_WIDTH, d_conv)), full((1, d_conv)),
                  full((len(POOL_WINDOWS), LANES, LANES)), full((1, d_conv)), full((1, d_conv)),
                  full((2 * d_conv, d)), full((1, d)), full((d, LANES))],
        out_specs=[pl.BlockSpec((MIX_TILE, d), lambda s: (s, 0)),
                   pl.BlockSpec((SUBLANES, MIX_TILE), lambda s: (0, jnp.maximum(s - 1, 0))),
                   pl.BlockSpec(memory_space=pl.ANY), smem, smem, smem, smem, smem, smem],
        out_shape=[jax.ShapeDtypeStruct((t + MIX_TILE, d), jnp.float32),
                   jax.ShapeDtypeStruct((SUBLANES, t), jnp.float32),
                   jax.ShapeDtypeStruct((n_chunks, CHUNK, d), bf),
                   jax.ShapeDtypeStruct((n_tiles * TAB_STRIDE,), i32),
                   jax.ShapeDtypeStruct((max_work,), i32),
                   jax.ShapeDtypeStruct((max_work,), i32),
                   jax.ShapeDtypeStruct((max_work,), i32),
                   jax.ShapeDtypeStruct((max_work,), i32),
                   jax.ShapeDtypeStruct((1,), i32)],
        scratch_shapes=[pltpu.VMEM((MIX_TILE + HALO, d_conv), jnp.float32),
                        pltpu.VMEM((MIX_TILE + HALO, d_conv), jnp.float32),
                        pltpu.VMEM((2, MIX_TILE, d), bf),
                        pltpu.VMEM((MIX_TILE, d), jnp.float32),
                        pltpu.VMEM((STAGE_ROWS, MIX_TILE), bf),
                        pltpu.VMEM((3, STAGE_CHUNKS + SEGMENT_CHUNKS, CHUNK, d), bf),
                        pltpu.VMEM((WAIT_GROUP, CHUNK, d), bf),
                        pltpu.VMEM((SUBLANES, LANES), i32),
                        pltpu.SMEM((2, SUBLANES, LANES), i32),
                        pltpu.SMEM((N_EXPERTS,), i32),
                        pltpu.SMEM((1,), i32),
                        pltpu.SemaphoreType.DMA,
                        pltpu.SemaphoreType.DMA],
        compiler_params=pltpu.CompilerParams(
            dimension_semantics=("arbitrary",), vmem_limit_bytes=VMEM_LIMIT),
        name="mixer",
    )(x2, x2, meta_tokens, row(norm_mix), w_in[0].astype(bf), conv_w[0],
      row(norm_conv_out), pool_w[0].astype(bf), row(pool_scale), row(norm_pool_out),
      w_out[0].astype(bf), row(norm_ffn), wr)

    anyspec = pl.BlockSpec(memory_space=pl.ANY)
    f32 = jnp.float32
    ys = pl.pallas_call(
        functools.partial(_moe_kernel, d_exp=d_exp, tile=MOE_TILE),
        grid_spec=pltpu.PrefetchScalarGridSpec(
            num_scalar_prefetch=5,
            grid=(1,),
            in_specs=[anyspec, anyspec, anyspec, anyspec],
            out_specs=anyspec,
            scratch_shapes=[pltpu.VMEM((3, MOE_TILE, d), bf), pltpu.VMEM((2, MOE_TILE, d), bf),
                            pltpu.VMEM((2, d, d_exp), f32), pltpu.VMEM((2, d, d_exp), f32),
                            pltpu.VMEM((2, d_exp, d), f32),
                            pltpu.VMEM((d, 2 * d_exp), bf), pltpu.VMEM((d_exp, d), bf),
                            pltpu.SemaphoreType.DMA((3,)), pltpu.SemaphoreType.DMA((2,)),
                            pltpu.SemaphoreType.DMA((2,))]),
        out_shape=jax.ShapeDtypeStruct((n_chunks * CHUNK, d), bf),
        compiler_params=pltpu.CompilerParams(
            dimension_semantics=("arbitrary",), vmem_limit_bytes=VMEM_LIMIT),
        name="moe",
    )(wexp, wblk, wnxt, wsub, nwork, xs.reshape(n_chunks * CHUNK, d), w_gate[0], w_up[0], w_down[0])

    out = pl.pallas_call(
        functools.partial(_combine_kernel, tm=MIX_TILE, cap_chunks=cap // CHUNK, n_tiles=n_tiles),
        grid_spec=pltpu.PrefetchScalarGridSpec(
            num_scalar_prefetch=1,
            grid=(1,),
            in_specs=[anyspec, anyspec, pl.BlockSpec((1, d), lambda s, tb: (0, 0)), anyspec],
            out_specs=anyspec,
            scratch_shapes=[pltpu.VMEM((2, MIX_TILE, d), f32), pltpu.VMEM((2, SUBLANES, MIX_TILE), f32),
                            pltpu.VMEM((2, MIX_TILE, d), f32),
                            pltpu.VMEM((3, STAGE_CHUNKS, CHUNK, d), bf),
                            pltpu.SemaphoreType.DMA((2,)), pltpu.SemaphoreType.DMA((2,)),
                            pltpu.SemaphoreType.DMA((3,))]),
        out_shape=jax.ShapeDtypeStruct((t, d), jnp.float32),
        compiler_params=pltpu.CompilerParams(
            dimension_semantics=("arbitrary",), vmem_limit_bytes=VMEM_LIMIT),
        name="combine",
    )(tab, h1, route, row(final_norm), ys.reshape(n_chunks, CHUNK, d))
    return out.reshape(bsz, seq, d)
```

```python
import functools

import jax
import jax.numpy as jnp
from jax import lax
from jax.experimental import pallas as pl
from jax.experimental.pallas import tpu as pltpu

EPS = 1e-6
N_META = 16
CONV_WIDTH = 3
POOL_WINDOWS = (2, 4, 8, 16)
N_EXPERT_GROUPS = 4
EXPERTS_PER_GROUP = 8
N_EXPERTS = N_EXPERT_GROUPS * EXPERTS_PER_GROUP
LANES = 128
SUBLANES = 8
HALO = 16
CHUNK = 16

MIX_TILE = 512
MOE_TILE = 1024
MOE_SUB = 256
MOE_SUB_CHUNKS = MOE_SUB // CHUNK
MOE_CHUNKS = MOE_TILE // CHUNK
MOE_CHUNKS_LOG2 = MOE_CHUNKS.bit_length() - 1
VMEM_LIMIT = 56 * 1024 * 1024

STAGE_ROWS = 2 * MIX_TILE + N_EXPERTS * CHUNK
STAGE_CHUNKS = STAGE_ROWS // CHUNK
COMMON_ROWS = 2 * MIX_TILE + N_EXPERTS * CHUNK // 2
PERM_ROWS = 512
TAB_STRIDE = 2 * N_EXPERTS + 1
UNROLLED_CHUNKS = 5
SEGMENT_CHUNKS = 5
WAIT_GROUP = 16
WAIT_GROUP_LOG2 = WAIT_GROUP.bit_length() - 1


def _rms(x, g):
    return x * lax.rsqrt(jnp.mean(x * x, axis=-1, keepdims=True) + EPS) * g


def _route(lt):
    tm = lt.shape[1]
    sub = lax.broadcasted_iota(jnp.int32, (SUBLANES, tm), 0)
    neg = jnp.float32(-jnp.inf)
    big = jnp.int32(1 << 20)
    lg = jnp.where(sub < N_EXPERT_GROUPS, lt[N_EXPERTS:N_EXPERTS + SUBLANES], neg)
    gmax = jnp.max(lg, axis=0, keepdims=True)
    g1 = 1.0 / jnp.sum(jnp.exp(lg - gmax), axis=0, keepdims=True)
    gsel = jnp.min(jnp.where(lg == gmax, sub, big), axis=0, keepdims=True)
    le = lt[(N_EXPERT_GROUPS - 1) * SUBLANES:N_EXPERT_GROUPS * SUBLANES]
    for g in range(N_EXPERT_GROUPS - 2, -1, -1):
        le = jnp.where(gsel == g, lt[g * SUBLANES:(g + 1) * SUBLANES], le)
    m1 = jnp.max(le, axis=0, keepdims=True)
    pe = jnp.exp(le - m1)
    i1 = jnp.min(jnp.where(le == m1, sub, big), axis=0, keepdims=True)
    pe2 = jnp.where(sub == i1, -1.0, pe)
    p2 = jnp.max(pe2, axis=0, keepdims=True)
    i2 = jnp.min(jnp.where(pe2 == p2, sub, big), axis=0, keepdims=True)
    inv = 1.0 / (1.0 + p2)
    return (gsel * EXPERTS_PER_GROUP + i1, gsel * EXPERTS_PER_GROUP + i2,
            g1 * inv, g1 * p2 * inv)


def _derived_zero(x):
    bits = pltpu.bitcast(x, jnp.int32)
    return lax.shift_right_logical(lax.shift_right_logical(bits, 16), 16).astype(jnp.float32)


def _copy_chunks(src_ref, src_chunk, dst_ref, dst_chunk, sem, n=1):
    return pltpu.make_async_copy(src_ref.at[pl.ds(src_chunk, n)],
                                 dst_ref.at[pl.ds(dst_chunk, n)], sem)


def _drain(n_chunks, src_ref, dst_ref, sem):
    def group_body(k, c):
        _copy_chunks(src_ref, 0, dst_ref, 0, sem, WAIT_GROUP).wait()
        return c
    lax.fori_loop(0, lax.shift_right_logical(n_chunks, WAIT_GROUP_LOG2), group_body, 0)

    def one_body(k, c):
        _copy_chunks(src_ref, 0, dst_ref, 0, sem).wait()
        return c
    lax.fori_loop(0, n_chunks & (WAIT_GROUP - 1), one_body, 0)


def _issue_head(head, src_ref, src0, dst_ref, dst0, sem):
    pairs = lax.shift_right_logical(head, 1)
    for k in range(UNROLLED_CHUNKS // 2):
        @pl.when(k < pairs)
        def _():
            _copy_chunks(src_ref, src0 + 2 * k, dst_ref, dst0 + 2 * k, sem, 2).start()

    @pl.when((head & 1) == 1)
    def _():
        _copy_chunks(src_ref, src0 + 2 * pairs, dst_ref, dst0 + 2 * pairs, sem).start()


def _issue_rest(first, nch, src_ref, src0, dst_ref, dst0, sem):
    def body(k, c):
        _copy_chunks(src_ref, src0 + k, dst_ref, dst0 + k, sem).start()
        return c
    lax.fori_loop(first, nch, body, 0)


def _mixer_kernel(x_ref, xn_ref, meta_ref, nmix_ref, win_ref, convw_ref, nconv_ref, poolw_ref,
                  pscale_ref, npool_ref, wout_ref, nffn_ref, wr_ref, wg_ref, wu_ref, wd_ref,
                  h1_ref, route_ref, xs_ref, tab_ref, wexp_ref, wblk_ref, wnxt_ref, wsub_ref, nwork_ref,
                  wgu_bf_ref, wd_bf_ref,
                  extu_ref, extp_ref, hn_ref, h1s_ref, perm_ref, stage_ref, zero_ref, info_v_ref,
                  info_s_ref, run_ref, pend_ref, sem_info, sem_out,
                  *, tm, d_conv, cap_chunks, tiles_per_seq, max_work):
    s = pl.program_id(0)
    n_tiles = pl.num_programs(0) - 1
    cur = s % 2
    prv = 1 - cur
    st_sort = (s + 2) % 3
    st_disp = (s + 1) % 3
    nmix = nmix_ref[...]

    def info_copy(slot):
        return pltpu.make_async_copy(info_v_ref, info_s_ref.at[slot], sem_info)

    @pl.when(s == 0)
    def _():
        pend_ref[0] = 0
        zero_ref[...] = jnp.zeros_like(zero_ref)
        stage_ref[...] = jnp.zeros_like(stage_ref)
        h1s_ref[...] = jnp.zeros_like(h1s_ref)
        hn_ref[1] = _rms(x_ref[...], nmix).astype(jnp.bfloat16)
        info_v_ref[...] = jnp.zeros_like(info_v_ref)
        info_copy(0).start()
        info_copy(0).wait()
        info_copy(1).start()

        def init(e, c):
            run_ref[e] = 0
            return c
        lax.fori_loop(0, N_EXPERTS, init, 0)

    @pl.when(s % tiles_per_seq == 0)
    def _():
        hm = _rms(meta_ref[...], nmix).astype(jnp.bfloat16)
        pm = jnp.dot(hm, win_ref[:, d_conv:], preferred_element_type=jnp.float32)
        extu_ref[0:HALO, :] = pm[:, 0:d_conv] * pm[:, d_conv:2 * d_conv]
        extp_ref[0:HALO, :] = pm[:, 2 * d_conv:]

    info_copy(cur).wait()

    hn_ref[0] = hn_ref[1]
    hn = hn_ref[0]
    proj_bc = jnp.dot(hn, win_ref[:, 0:2 * d_conv], preferred_element_type=jnp.float32)

    hfp = _rms(h1s_ref[...], nffn_ref[...]).astype(jnp.bfloat16)
    lt = jnp.dot(hfp, wr_ref[...], preferred_element_type=jnp.float32).T
    e0, e1, gate0, gate1 = _route(lt)
    esub = lax.broadcasted_iota(jnp.int32, (N_EXPERTS, tm), 0)
    dead = (s == 0).astype(jnp.int32)
    oh0 = esub == e0 - dead * N_EXPERTS
    oh1 = esub == e1 - dead * N_EXPERTS
    oh = jnp.where(oh0, 1.0, jnp.where(oh1, 1.0, 0.0))
    tok = lax.broadcasted_iota(jnp.int32, (N_EXPERTS, tm), 1)
    ct = oh
    span = 1
    while span < tm:
        ct = ct + jnp.where(tok >= span, pltpu.roll(ct, span, 1), 0.0)
        span *= 2
    ct = ct - oh
    n_col = jnp.sum(oh, axis=1, keepdims=True)
    nch_col = jnp.floor((n_col + (CHUNK - 1)) * (1.0 / CHUNK))
    e_sub = lax.broadcasted_iota(jnp.int32, (N_EXPERTS, LANES), 0)
    l_lane = lax.broadcasted_iota(jnp.int32, (N_EXPERTS, LANES), 1)
    nch_row = jnp.sum(jnp.where(e_sub == l_lane, nch_col, 0.0), axis=0, keepdims=True)
    coff_col = jnp.sum(jnp.where(l_lane < e_sub, nch_row, 0.0), axis=1, keepdims=True)
    coff_row = jnp.sum(jnp.where(e_sub == l_lane, coff_col, 0.0), axis=0, keepdims=True)
    info_v_ref[...] = jnp.concatenate(
        [nch_row, coff_row, jnp.zeros((SUBLANES - 2, LANES), jnp.float32)], axis=0
    ).astype(jnp.int32)
    info_copy(prv).start()

    posbase = ct + coff_col * CHUNK
    pos0 = jnp.sum(jnp.where(oh0, posbase, 0.0), axis=0, keepdims=True)
    pos1 = jnp.sum(jnp.where(oh1, posbase, 0.0), axis=0, keepdims=True)
    route_ref[...] = jnp.concatenate(
        [pos0, pos1, gate0, gate1, jnp.zeros((SUBLANES - 4, tm), jnp.float32)], axis=0)

    pos0i = pos0.astype(jnp.int32)
    pos1i = pos1.astype(jnp.int32)
    for r0 in range(0, STAGE_ROWS, PERM_ROWS):
        r = lax.broadcasted_iota(jnp.int32, (PERM_ROWS, tm), 0) + r0
        perm_ref[r0:r0 + PERM_ROWS, :] = jnp.where(
            r == pos0i, 1.0, jnp.where(r == pos1i, 1.0, 0.0)).astype(jnp.bfloat16)

    x = x_ref[...]
    proj_vv = jnp.dot(hn, win_ref[:, 2 * d_conv:], preferred_element_type=jnp.float32)
    b_gate = proj_bc[:, 0:d_conv]
    v_pool = proj_vv[:, d_conv:]
    extu_ref[HALO:, :] = proj_bc[:, d_conv:] * proj_vv[:, 0:d_conv]
    extp_ref[HALO:, :] = v_pool

    hn_next = _rms(xn_ref[...], nmix).astype(jnp.bfloat16)
    anchor = proj_vv[tm - 1:tm, :] + hn_next[tm - 1:tm, :].astype(jnp.float32)
    hf_late = hfp + _derived_zero(anchor).astype(jnp.bfloat16)
    for r0 in range(0, STAGE_ROWS, PERM_ROWS):
        rows = jnp.dot(perm_ref[r0:r0 + PERM_ROWS, :], hf_late, preferred_element_type=jnp.float32)
        stage_ref[st_sort, r0 // CHUNK:(r0 + PERM_ROWS) // CHUNK] = (
            rows.astype(jnp.bfloat16).reshape(PERM_ROWS // CHUNK, CHUNK, x_ref.shape[-1]))

    conv = convw_ref[CONV_WIDTH - 1:CONV_WIDTH, :] * extu_ref[HALO:, :]
    for k in range(CONV_WIDTH - 1):
        off = HALO - (CONV_WIDTH - 1 - k)
        conv = conv + convw_ref[k:k + 1, :] * extu_ref[off:off + tm, :]
    y_conv = _rms(b_gate * conv, nconv_ref[...])

    mixed = []
    for g, w in enumerate(POOL_WINDOWS):
        cols = slice(g * LANES, (g + 1) * LANES)
        acc = extp_ref[:, cols]
        span = 1
        while span < w:
            acc = acc + pltpu.roll(acc, span, 0)
            span *= 2
        pooled = acc[HALO:] * (1.0 / w) - v_pool[:, cols]
        mixed.append(jnp.dot(pooled.astype(jnp.bfloat16), poolw_ref[g],
                             preferred_element_type=jnp.float32))
    y_pool = _rms(jnp.concatenate(mixed, axis=-1) * pscale_ref[...], npool_ref[...])

    y = jnp.concatenate([y_conv, y_pool], axis=-1).astype(jnp.bfloat16)
    h1 = x + jnp.dot(y, wout_ref[...], preferred_element_type=jnp.float32)
    h1_ref[...] = h1
    h1s_ref[...] = h1

    d_exp = wg_ref.shape[-1]
    wgu_bf_ref[0, :, :d_exp] = wg_ref[0].astype(jnp.bfloat16)
    wgu_bf_ref[0, :, d_exp:] = wu_ref[0].astype(jnp.bfloat16)
    wd_bf_ref[0] = wd_ref[0].astype(jnp.bfloat16)
    hn_ref[1] = hn_next

    extu_ref[0:HALO, :] = extu_ref[tm:tm + HALO, :]
    extp_ref[0:HALO, :] = extp_ref[tm:tm + HALO, :]

    pending = pend_ref[0]
    tile2 = jnp.maximum(s - 2, 0)
    tot = 0
    issued = 0
    most = 0
    plan = []
    for e in range(N_EXPERTS):
        nch = info_s_ref[cur, 0, e]
        base = run_ref[e]
        tab_ref[tile2 * TAB_STRIDE + e] = base
        tab_ref[tile2 * TAB_STRIDE + N_EXPERTS + e] = nch
        run_ref[e] = base + nch
        plan.append((nch > 0, info_s_ref[cur, 1, e], e * cap_chunks + base))
        tot = tot + nch
        issued = issued + jnp.where(nch > 0, SEGMENT_CHUNKS, 0) + jnp.maximum(nch - SEGMENT_CHUNKS, 0)
        most = jnp.maximum(most, nch)
    tab_ref[tile2 * TAB_STRIDE + 2 * N_EXPERTS] = tot
    pend_ref[0] = issued
    _drain(pending, zero_ref, xs_ref, sem_out)
    for nonempty, src0, dst0 in plan:
        @pl.when(nonempty)
        def _():
            _copy_chunks(stage_ref.at[st_disp], src0, xs_ref, dst0, sem_out, SEGMENT_CHUNKS).start()

    @pl.when(most > SEGMENT_CHUNKS)
    def _():
        def body(e, c):
            _issue_rest(SEGMENT_CHUNKS, info_s_ref[cur, 0, e],
                        stage_ref.at[st_disp], info_s_ref[cur, 1, e],
                        xs_ref, e * cap_chunks + tab_ref[tile2 * TAB_STRIDE + e], sem_out)
            return c
        lax.fori_loop(0, N_EXPERTS, body, 0)

    @pl.when(s == n_tiles)
    def _():
        info_copy(prv).wait()
        _drain(pend_ref[0], zero_ref, xs_ref, sem_out)
        sem_last = sem_out

        def last_body(e, tot):
            nch = info_s_ref[prv, 0, e]
            base = run_ref[e]
            tab_ref[(s - 1) * TAB_STRIDE + e] = base
            tab_ref[(s - 1) * TAB_STRIDE + N_EXPERTS + e] = nch
            _issue_rest(0, nch, stage_ref.at[st_sort], info_s_ref[prv, 1, e],
                        xs_ref, e * cap_chunks + base, sem_last)
            run_ref[e] = base + nch
            return tot + nch
        last = lax.fori_loop(0, N_EXPERTS, last_body, 0)
        tab_ref[(s - 1) * TAB_STRIDE + 2 * N_EXPERTS] = last
        _drain(last, zero_ref, xs_ref, sem_last)

        subs_per_tile = MOE_TILE // MOE_SUB

        def pad_body(e, carry):
            tot, w = carry
            n = run_ref[e]
            subs = (n + (MOE_SUB_CHUNKS - 1)) // MOE_SUB_CHUNKS
            tiles = (subs + (subs_per_tile - 1)) // subs_per_tile
            nch = subs * MOE_SUB_CHUNKS - n

            def chunk_body(k, c):
                _copy_chunks(zero_ref, 0, xs_ref, e * cap_chunks + n + k, sem_last).start()
                return c
            lax.fori_loop(0, nch, chunk_body, 0)

            def work_body(j, c):
                wexp_ref[w + j] = e
                wblk_ref[w + j] = e * (cap_chunks // MOE_CHUNKS) + j
                wsub_ref[w + j] = jnp.minimum(subs - j * subs_per_tile, subs_per_tile)
                return c
            lax.fori_loop(0, tiles, work_body, 0)
            return tot + nch, w + tiles
        n_pad, n_work = lax.fori_loop(0, N_EXPERTS, pad_body, (0, 0))
        _drain(n_pad, zero_ref, xs_ref, sem_last)
        nwork_ref[0] = n_work

        def next_body(k, nxt):
            w = n_work - 1 - k
            wnxt_ref[w] = nxt
            prev = wexp_ref[jnp.maximum(w - 1, 0)]
            return jnp.where(prev != wexp_ref[w], wexp_ref[w], nxt)
        lax.fori_loop(0, n_work, next_body, wexp_ref[n_work - 1])

        def fill_body(w, c):
            wexp_ref[w] = 0
            wblk_ref[w] = 0
            wnxt_ref[w] = 0
            wsub_ref[w] = 0
            return c
        lax.fori_loop(n_work, max_work, fill_body, 0)


def _moe_kernel(exp_ref, blk_ref, nxt_ref, sub_ref, n_ref, xs_ref, wgu_ref, wd_ref, ys_ref,
                xbuf_ref, ybuf_ref, wgu_buf_ref, wd_buf_ref,
                sem_x, sem_y, sem_w, *, d_exp, tile):
    n = n_ref[0]
    sizes = range(1, tile // MOE_SUB + 1)

    def x_copy(w, slot):
        return pltpu.make_async_copy(xs_ref.at[pl.ds(blk_ref[w] * tile, tile)],
                                     xbuf_ref.at[slot], sem_x.at[slot])

    def y_copy(w, slot):
        return pltpu.make_async_copy(ybuf_ref.at[slot],
                                     ys_ref.at[pl.ds(blk_ref[w] * tile, tile)], sem_y.at[slot])

    def w_copies(e, slot):
        return (pltpu.make_async_copy(wgu_ref.at[e], wgu_buf_ref.at[slot], sem_w.at[slot]),
                pltpu.make_async_copy(wd_ref.at[e], wd_buf_ref.at[slot], sem_w.at[slot]))

    for c in w_copies(exp_ref[0], 0):
        c.start()
    x_copy(0, 0).start()

    @pl.when(n > 1)
    def _():
        x_copy(1, 1).start()

    def item(w, run):
        e = exp_ref[w]
        first = (w == 0) | (exp_ref[jnp.maximum(w - 1, 0)] != e)
        run = run + jnp.where(first, 1, 0)
        wslot = (run - 1) % 2
        xslot = w % 3
        yslot = w % 2

        @pl.when(first)
        def _():
            for c in w_copies(e, wslot):
                c.wait()

            @pl.when(nxt_ref[w] != e)
            def _():
                for c in w_copies(nxt_ref[w], 1 - wslot):
                    c.start()

        x_copy(w, xslot).wait()

        @pl.when(w + 2 < n)
        def _():
            x_copy(w + 2, (w + 2) % 3).start()

        @pl.when(w >= 2)
        def _():
            y_copy(w - 2, yslot).wait()

        for k in sizes:
            @pl.when(sub_ref[w] == k)
            def _():
                rows = k * MOE_SUB
                gu = jnp.dot(xbuf_ref[xslot, 0:rows], wgu_buf_ref[wslot],
                             preferred_element_type=jnp.float32)
                g = gu[:, :d_exp]
                a = g * jax.nn.sigmoid(g) * gu[:, d_exp:]
                ybuf_ref[yslot, 0:rows] = jnp.dot(
                    a.astype(jnp.bfloat16), wd_buf_ref[wslot],
                    preferred_element_type=jnp.float32).astype(jnp.bfloat16)
        y_copy(w, yslot).start()
        return run
    lax.fori_loop(0, n, item, 0)

    @pl.when(n >= 2)
    def _():
        y_copy(n - 2, n % 2).wait()
    y_copy(n - 1, (n - 1) % 2).wait()


def _combine_kernel(tab_ref, h1_ref, route_ref, fn_ref, ys_ref, out_ref,
                    h1buf_ref, rbuf_ref, obuf_ref, stage_ref, sem_in, sem_out, sem,
                    *, tm, cap_chunks, n_tiles):
    d = obuf_ref.shape[-1]

    def in_copies(j, b):
        rows = pl.ds(j * tm, tm)
        return (pltpu.make_async_copy(h1_ref.at[rows], h1buf_ref.at[b], sem_in.at[b]),
                pltpu.make_async_copy(route_ref.at[:, rows], rbuf_ref.at[b], sem_in.at[b]))

    def out_copy(j, b):
        return pltpu.make_async_copy(obuf_ref.at[b], out_ref.at[pl.ds(j * tm, tm)], sem_out.at[b])

    stage_ref[...] = jnp.zeros_like(stage_ref)
    for j in range(2):
        def body(e, coff):
            nch = tab_ref[j * TAB_STRIDE + N_EXPERTS + e]
            _issue_rest(0, nch, ys_ref, e * cap_chunks + tab_ref[j * TAB_STRIDE + e],
                        stage_ref.at[j], coff, sem.at[j])
            return coff + nch
        lax.fori_loop(0, N_EXPERTS, body, 0)
        for c in in_copies(j, j):
            c.start()

    def tile_body(s, carry):
        slot = s % 3
        ahead = (s + 2) % 3
        b = s % 2
        n_chunks = tab_ref[s * TAB_STRIDE + 2 * N_EXPERTS]
        _drain(n_chunks, ys_ref, stage_ref.at[slot], sem.at[slot])
        for c in in_copies(s, b):
            c.wait()

        @pl.when(s >= 2)
        def _():
            out_copy(s - 2, b).wait()

        route = jnp.concatenate(
            [rbuf_ref[b], jnp.zeros((LANES - SUBLANES, tm), jnp.float32)], axis=0).T
        pos0 = route[:, 0:1].astype(jnp.int32)
        pos1 = route[:, 1:2].astype(jnp.int32)
        gate0 = route[:, 2:3]
        gate1 = route[:, 3:4]

        def combine(n_rows):
            r = lax.broadcasted_iota(jnp.int32, (tm, n_rows), 1)
            pw = jnp.where(r == pos0, gate0, jnp.where(r == pos1, gate1, 0.0)).astype(jnp.bfloat16)
            rows = stage_ref[slot, 0:n_rows // CHUNK].reshape(n_rows, d)
            moe = jnp.dot(pw, rows, preferred_element_type=jnp.float32)
            obuf_ref[b] = _rms(h1buf_ref[b] + moe, fn_ref[...])

        @pl.when(n_chunks <= COMMON_ROWS // CHUNK)
        def _():
            combine(COMMON_ROWS)

        @pl.when(n_chunks > COMMON_ROWS // CHUNK)
        def _():
            combine(STAGE_ROWS)

        out_copy(s, b).start()

        @pl.when(s + 2 < n_tiles)
        def _():
            for c in in_copies(s + 2, b):
                c.start()
        nxt = jnp.minimum(s + 2, n_tiles - 1)
        has_next = jnp.where(s + 2 < n_tiles, 1, 0)
        coff = 0
        most = 0
        plan = []
        for e in range(N_EXPERTS):
            nch = tab_ref[nxt * TAB_STRIDE + N_EXPERTS + e] * has_next
            plan.append((jnp.minimum(nch, UNROLLED_CHUNKS),
                         e * cap_chunks + tab_ref[nxt * TAB_STRIDE + e], coff))
            coff = coff + nch
            most = jnp.maximum(most, nch)
        for head, src0, dst0 in plan:
            _issue_head(head, ys_ref, src0, stage_ref.at[ahead], dst0, sem.at[ahead])

        @pl.when(most > UNROLLED_CHUNKS)
        def _():
            def body(e, coff):
                nch = tab_ref[nxt * TAB_STRIDE + N_EXPERTS + e]
                _issue_rest(UNROLLED_CHUNKS, nch, ys_ref,
                            e * cap_chunks + tab_ref[nxt * TAB_STRIDE + e],
                            stage_ref.at[ahead], coff, sem.at[ahead])
                return coff + nch
            lax.fori_loop(0, N_EXPERTS, body, 0)
        return carry
    lax.fori_loop(0, n_tiles, tile_body, 0)

    out_copy(n_tiles - 2, n_tiles % 2).wait()
    out_copy(n_tiles - 1, (n_tiles - 1) % 2).wait()


def kernel(x, meta_tokens, norm_mix, w_in, conv_w, norm_conv_out, pool_w, pool_scale,
           norm_pool_out, w_out, norm_ffn, w_router_group, w_router_expert,
           w_gate, w_up, w_down, final_norm):
    bsz, seq, d = x.shape
    d_conv = conv_w.shape[-1]
    d_exp = w_gate.shape[-1]
    n_exp = w_gate.shape[1]
    assert meta_tokens.shape[0] == N_META == HALO and n_exp == N_EXPERTS
    assert seq % MIX_TILE == 0
    bf = jnp.bfloat16
    i32 = jnp.int32
    t = bsz * seq
    n_tiles = t // MIX_TILE
    cap = -(-(t + CHUNK * n_tiles) // MOE_TILE) * MOE_TILE + MOE_TILE
    n_chunks = N_EXPERTS * cap // CHUNK
    max_work = (2 * t + N_EXPERTS * (CHUNK - 1) * n_tiles) // MOE_TILE + N_EXPERTS
    row = lambda a: a.reshape(1, -1)
    wr = jnp.concatenate(
        [w_router_expert[0], w_router_group[0],
         jnp.zeros((d, LANES - N_EXPERTS - N_EXPERT_GROUPS), jnp.float32)], axis=1).astype(bf)

    full = lambda shape: pl.BlockSpec(shape, lambda s: (0,) * len(shape))
    last = n_tiles - 1
    smem = pl.BlockSpec(memory_space=pltpu.SMEM)
    x2 = x.reshape(t, d)
    wlast = n_exp - 1
    wspec = lambda rows, cols: pl.BlockSpec((1, rows, cols), lambda s: (jnp.minimum(s, wlast), 0, 0))
    assert n_tiles + 1 >= n_exp
    h1, route, xs, tab, wexp, wblk, wnxt, wsub, nwork, wgu_bf, wd_bf = pl.pallas_call(
        functools.partial(_mixer_kernel, tm=MIX_TILE, d_conv=d_conv, cap_chunks=cap // CHUNK,
                          tiles_per_seq=seq // MIX_TILE, max_work=max_work),
        grid=(n_tiles + 1,),
        in_specs=[pl.BlockSpec((MIX_TILE, d), lambda s: (jnp.minimum(s, last), 0)),
                  pl.BlockSpec((MIX_TILE, d), lambda s: (jnp.minimum(s + 1, last), 0)),
                  full((N_META, d)), full((1, d)), full((d, 4 * d_conv)),
                  full((CONV_WIDTH, d_conv)), full((1, d_conv)),
                  full((len(POOL_WINDOWS), LANES, LANES)), full((1, d_conv)), full((1, d_conv)),
                  full((2 * d_conv, d)), full((1, d)), full((d, LANES)),
                  wspec(d, d_exp), wspec(d, d_exp), wspec(d_exp, d)],
        out_specs=[pl.BlockSpec((MIX_TILE, d), lambda s: (s, 0)),
                   pl.BlockSpec((SUBLANES, MIX_TILE), lambda s: (0, jnp.maximum(s - 1, 0))),
                   pl.BlockSpec(memory_space=pl.ANY), smem, smem, smem, smem, smem, smem,
                   wspec(d, 2 * d_exp), wspec(d_exp, d)],
        out_shape=[jax.ShapeDtypeStruct((t + MIX_TILE, d), jnp.float32),
                   jax.ShapeDtypeStruct((SUBLANES, t), jnp.float32),
                   jax.ShapeDtypeStruct((n_chunks, CHUNK, d), bf),
                   jax.ShapeDtypeStruct((n_tiles * TAB_STRIDE,), i32),
                   jax.ShapeDtypeStruct((max_work,), i32),
                   jax.ShapeDtypeStruct((max_work,), i32),
                   jax.ShapeDtypeStruct((max_work,), i32),
                   jax.ShapeDtypeStruct((max_work,), i32),
                   jax.ShapeDtypeStruct((1,), i32),
                   jax.ShapeDtypeStruct((n_exp, d, 2 * d_exp), bf),
                   jax.ShapeDtypeStruct((n_exp, d_exp, d), bf)],
        scratch_shapes=[pltpu.VMEM((MIX_TILE + HALO, d_conv), jnp.float32),
                        pltpu.VMEM((MIX_TILE + HALO, d_conv), jnp.float32),
                        pltpu.VMEM((2, MIX_TILE, d), bf),
                        pltpu.VMEM((MIX_TILE, d), jnp.float32),
                        pltpu.VMEM((STAGE_ROWS, MIX_TILE), bf),
                        pltpu.VMEM((3, STAGE_CHUNKS + SEGMENT_CHUNKS, CHUNK, d), bf),
                        pltpu.VMEM((WAIT_GROUP, CHUNK, d), bf),
                        pltpu.VMEM((SUBLANES, LANES), i32),
                        pltpu.SMEM((2, SUBLANES, LANES), i32),
                        pltpu.SMEM((N_EXPERTS,), i32),
                        pltpu.SMEM((1,), i32),
                        pltpu.SemaphoreType.DMA,
                        pltpu.SemaphoreType.DMA],
        compiler_params=pltpu.CompilerParams(
            dimension_semantics=("arbitrary",), vmem_limit_bytes=VMEM_LIMIT),
        name="mixer",
    )(x2, x2, meta_tokens, row(norm_mix), w_in[0].astype(bf), conv_w[0],
      row(norm_conv_out), pool_w[0].astype(bf), row(pool_scale), row(norm_pool_out),
      w_out[0].astype(bf), row(norm_ffn), wr, w_gate[0], w_up[0], w_down[0])

    anyspec = pl.BlockSpec(memory_space=pl.ANY)
    f32 = jnp.float32
    ys = pl.pallas_call(
        functools.partial(_moe_kernel, d_exp=d_exp, tile=MOE_TILE),
        grid_spec=pltpu.PrefetchScalarGridSpec(
            num_scalar_prefetch=5,
            grid=(1,),
            in_specs=[anyspec, anyspec, anyspec],
            out_specs=anyspec,
            scratch_shapes=[pltpu.VMEM((3, MOE_TILE, d), bf), pltpu.VMEM((2, MOE_TILE, d), bf),
                            pltpu.VMEM((2, d, 2 * d_exp), bf), pltpu.VMEM((2, d_exp, d), bf),
                            pltpu.SemaphoreType.DMA((3,)), pltpu.SemaphoreType.DMA((2,)),
                            pltpu.SemaphoreType.DMA((2,))]),
        out_shape=jax.ShapeDtypeStruct((n_chunks * CHUNK, d), bf),
        compiler_params=pltpu.CompilerParams(
            dimension_semantics=("arbitrary",), vmem_limit_bytes=VMEM_LIMIT),
        name="moe",
    )(wexp, wblk, wnxt, wsub, nwork, xs.reshape(n_chunks * CHUNK, d), wgu_bf, wd_bf)

    out = pl.pallas_call(
        functools.partial(_combine_kernel, tm=MIX_TILE, cap_chunks=cap // CHUNK, n_tiles=n_tiles),
        grid_spec=pltpu.PrefetchScalarGridSpec(
            num_scalar_prefetch=1,
            grid=(1,),
            in_specs=[anyspec, anyspec, pl.BlockSpec((1, d), lambda s, tb: (0, 0)), anyspec],
            out_specs=anyspec,
            scratch_shapes=[pltpu.VMEM((2, MIX_TILE, d), f32), pltpu.VMEM((2, SUBLANES, MIX_TILE), f32),
                            pltpu.VMEM((2, MIX_TILE, d), f32),
                            pltpu.VMEM((3, STAGE_CHUNKS, CHUNK, d), bf),
                            pltpu.SemaphoreType.DMA((2,)), pltpu.SemaphoreType.DMA((2,)),
                            pltpu.SemaphoreType.DMA((3,))]),
        out_shape=jax.ShapeDtypeStruct((t, d), jnp.float32),
        compiler_params=pltpu.CompilerParams(
            dimension_semantics=("arbitrary",), vmem_limit_bytes=VMEM_LIMIT),
        name="combine",
    )(tab, h1, route, row(final_norm), ys.reshape(n_chunks, CHUNK, d))
    return out.reshape(bsz, seq, d)
```

```python
import functools

import jax
import jax.numpy as jnp
from jax import lax
from jax.experimental import pallas as pl
from jax.experimental.pallas import tpu as pltpu

EPS = 1e-6
N_META = 16
CONV_WIDTH = 3
POOL_WINDOWS = (2, 4, 8, 16)
N_EXPERT_GROUPS = 4
EXPERTS_PER_GROUP = 8
N_EXPERTS = N_EXPERT_GROUPS * EXPERTS_PER_GROUP
LANES = 128
SUBLANES = 8
HALO = 16
CHUNK = 16

MIX_TILE = 512
MOE_TILE = 1024
MOE_SUB = 256
MOE_SUB_CHUNKS = MOE_SUB // CHUNK
MOE_CHUNKS = MOE_TILE // CHUNK
MOE_CHUNKS_LOG2 = MOE_CHUNKS.bit_length() - 1
VMEM_LIMIT = 56 * 1024 * 1024

STAGE_ROWS = 2 * MIX_TILE + N_EXPERTS * CHUNK
STAGE_CHUNKS = STAGE_ROWS // CHUNK
COMMON_ROWS = 2 * MIX_TILE + N_EXPERTS * CHUNK // 2
PERM_ROWS = COMMON_ROWS // 2
TAB_STRIDE = 2 * N_EXPERTS + 1
UNROLLED_CHUNKS = 5
SEGMENT_CHUNKS = 5
WAIT_GROUP = 16
WAIT_GROUP_LOG2 = WAIT_GROUP.bit_length() - 1


def _rms(x, g):
    return x * lax.rsqrt(jnp.mean(x * x, axis=-1, keepdims=True) + EPS) * g


def _route(lt):
    tm = lt.shape[1]
    sub = lax.broadcasted_iota(jnp.int32, (SUBLANES, tm), 0)
    neg = jnp.float32(-jnp.inf)
    big = jnp.int32(1 << 20)
    lg = jnp.where(sub < N_EXPERT_GROUPS, lt[N_EXPERTS:N_EXPERTS + SUBLANES], neg)
    gmax = jnp.max(lg, axis=0, keepdims=True)
    g1 = 1.0 / jnp.sum(jnp.exp(lg - gmax), axis=0, keepdims=True)
    gsel = jnp.min(jnp.where(lg == gmax, sub, big), axis=0, keepdims=True)
    le = lt[(N_EXPERT_GROUPS - 1) * SUBLANES:N_EXPERT_GROUPS * SUBLANES]
    for g in range(N_EXPERT_GROUPS - 2, -1, -1):
        le = jnp.where(gsel == g, lt[g * SUBLANES:(g + 1) * SUBLANES], le)
    m1 = jnp.max(le, axis=0, keepdims=True)
    pe = jnp.exp(le - m1)
    i1 = jnp.min(jnp.where(le == m1, sub, big), axis=0, keepdims=True)
    pe2 = jnp.where(sub == i1, -1.0, pe)
    p2 = jnp.max(pe2, axis=0, keepdims=True)
    i2 = jnp.min(jnp.where(pe2 == p2, sub, big), axis=0, keepdims=True)
    inv = 1.0 / (1.0 + p2)
    return (gsel * EXPERTS_PER_GROUP + i1, gsel * EXPERTS_PER_GROUP + i2,
            g1 * inv, g1 * p2 * inv)


def _derived_zero(x):
    bits = pltpu.bitcast(x, jnp.int32)
    return lax.shift_right_logical(lax.shift_right_logical(bits, 16), 16).astype(jnp.float32)


def _copy_chunks(src_ref, src_chunk, dst_ref, dst_chunk, sem, n=1):
    return pltpu.make_async_copy(src_ref.at[pl.ds(src_chunk, n)],
                                 dst_ref.at[pl.ds(dst_chunk, n)], sem)


def _drain(n_chunks, src_ref, dst_ref, sem):
    def group_body(k, c):
        _copy_chunks(src_ref, 0, dst_ref, 0, sem, WAIT_GROUP).wait()
        return c
    lax.fori_loop(0, lax.shift_right_logical(n_chunks, WAIT_GROUP_LOG2), group_body, 0)

    def one_body(k, c):
        _copy_chunks(src_ref, 0, dst_ref, 0, sem).wait()
        return c
    lax.fori_loop(0, n_chunks & (WAIT_GROUP - 1), one_body, 0)


def _issue_head(head, src_ref, src0, dst_ref, dst0, sem):
    pairs = lax.shift_right_logical(head, 1)
    for k in range(UNROLLED_CHUNKS // 2):
        @pl.when(k < pairs)
        def _():
            _copy_chunks(src_ref, src0 + 2 * k, dst_ref, dst0 + 2 * k, sem, 2).start()

    @pl.when((head & 1) == 1)
    def _():
        _copy_chunks(src_ref, src0 + 2 * pairs, dst_ref, dst0 + 2 * pairs, sem).start()


def _issue_rest(first, nch, src_ref, src0, dst_ref, dst0, sem):
    def body(k, c):
        _copy_chunks(src_ref, src0 + k, dst_ref, dst0 + k, sem).start()
        return c
    lax.fori_loop(first, nch, body, 0)


def _mixer_kernel(x_ref, xn_ref, meta_ref, nmix_ref, win_ref, convw_ref, nconv_ref, poolw_ref,
                  pscale_ref, npool_ref, wout_ref, nffn_ref, wr_ref,
                  h1_ref, route_ref, xs_ref, tab_ref, wexp_ref, wblk_ref, wnxt_ref, wsub_ref, nwork_ref,
                  extu_ref, extp_ref, hn_ref, h1s_ref, perm_ref, stage_ref, zero_ref, info_v_ref,
                  info_s_ref, run_ref, pend_ref, sem_info, sem_out,
                  *, tm, d_conv, cap_chunks, tiles_per_seq, max_work):
    s = pl.program_id(0)
    n_tiles = pl.num_programs(0) - 1
    cur = s % 2
    prv = 1 - cur
    st_sort = (s + 2) % 3
    st_disp = (s + 1) % 3
    nmix = nmix_ref[...]

    def info_copy(slot):
        return pltpu.make_async_copy(info_v_ref, info_s_ref.at[slot], sem_info)

    @pl.when(s == 0)
    def _():
        pend_ref[0] = 0
        zero_ref[...] = jnp.zeros_like(zero_ref)
        stage_ref[...] = jnp.zeros_like(stage_ref)
        h1s_ref[...] = jnp.zeros_like(h1s_ref)
        hn_ref[1] = _rms(x_ref[...], nmix).astype(jnp.bfloat16)
        info_v_ref[...] = jnp.zeros_like(info_v_ref)
        info_copy(0).start()
        info_copy(0).wait()
        info_copy(1).start()

        def init(e, c):
            run_ref[e] = 0
            return c
        lax.fori_loop(0, N_EXPERTS, init, 0)

    @pl.when(s % tiles_per_seq == 0)
    def _():
        hm = _rms(meta_ref[...], nmix).astype(jnp.bfloat16)
        pm = jnp.dot(hm, win_ref[:, d_conv:], preferred_element_type=jnp.float32)
        extu_ref[0:HALO, :] = pm[:, 0:d_conv] * pm[:, d_conv:2 * d_conv]
        extp_ref[0:HALO, :] = pm[:, 2 * d_conv:]

    info_copy(cur).wait()

    hn_ref[0] = hn_ref[1]
    hn = hn_ref[0]
    proj_bc = jnp.dot(hn, win_ref[:, 0:2 * d_conv], preferred_element_type=jnp.float32)

    hfp = _rms(h1s_ref[...], nffn_ref[...]).astype(jnp.bfloat16)
    lt = jnp.dot(hfp, wr_ref[...], preferred_element_type=jnp.float32).T
    e0, e1, gate0, gate1 = _route(lt)
    esub = lax.broadcasted_iota(jnp.int32, (N_EXPERTS, tm), 0)
    dead = (s == 0).astype(jnp.int32)
    oh0 = esub == e0 - dead * N_EXPERTS
    oh1 = esub == e1 - dead * N_EXPERTS
    oh = jnp.where(oh0, 1.0, jnp.where(oh1, 1.0, 0.0))
    tok = lax.broadcasted_iota(jnp.int32, (N_EXPERTS, tm), 1)
    ct = oh
    span = 1
    while span < tm:
        ct = ct + jnp.where(tok >= span, pltpu.roll(ct, span, 1), 0.0)
        span *= 2
    ct = ct - oh
    n_col = jnp.sum(oh, axis=1, keepdims=True)
    nch_col = jnp.floor((n_col + (CHUNK - 1)) * (1.0 / CHUNK))
    e_sub = lax.broadcasted_iota(jnp.int32, (N_EXPERTS, LANES), 0)
    l_lane = lax.broadcasted_iota(jnp.int32, (N_EXPERTS, LANES), 1)
    nch_row = jnp.sum(jnp.where(e_sub == l_lane, nch_col, 0.0), axis=0, keepdims=True)
    coff_col = jnp.sum(jnp.where(l_lane < e_sub, nch_row, 0.0), axis=1, keepdims=True)
    coff_row = jnp.sum(jnp.where(e_sub == l_lane, coff_col, 0.0), axis=0, keepdims=True)
    info_v_ref[...] = jnp.concatenate(
        [nch_row, coff_row, jnp.zeros((SUBLANES - 2, LANES), jnp.float32)], axis=0
    ).astype(jnp.int32)
    info_copy(prv).start()

    posbase = ct + coff_col * CHUNK
    pos0 = jnp.sum(jnp.where(oh0, posbase, 0.0), axis=0, keepdims=True)
    pos1 = jnp.sum(jnp.where(oh1, posbase, 0.0), axis=0, keepdims=True)
    route_t = jnp.concatenate(
        [pos0, pos1, gate0, gate1, jnp.zeros((LANES - 4, tm), jnp.float32)], axis=0)
    route_ref[...] = route_t.T

    pos0i = pos0.astype(jnp.int32)
    pos1i = pos1.astype(jnp.int32)
    n_sorted = jnp.sum(nch_col, axis=0, keepdims=True)[0, 0].astype(jnp.int32) * CHUNK

    def sort_matrix(r0, n_rows):
        r = lax.broadcasted_iota(jnp.int32, (n_rows, tm), 0) + r0
        return jnp.where(r == pos0i, 1.0, jnp.where(r == pos1i, 1.0, 0.0)).astype(jnp.bfloat16)

    for r0 in range(0, COMMON_ROWS, PERM_ROWS):
        perm_ref[r0:r0 + PERM_ROWS, :] = sort_matrix(r0, PERM_ROWS)

    x = x_ref[...]
    proj_vv = jnp.dot(hn, win_ref[:, 2 * d_conv:], preferred_element_type=jnp.float32)
    b_gate = proj_bc[:, 0:d_conv]
    v_pool = proj_vv[:, d_conv:]
    extu_ref[HALO:, :] = proj_bc[:, d_conv:] * proj_vv[:, 0:d_conv]
    extp_ref[HALO:, :] = v_pool

    hn_next = _rms(xn_ref[...], nmix).astype(jnp.bfloat16)
    anchor = proj_vv[tm - 1:tm, :] + hn_next[tm - 1:tm, :].astype(jnp.float32)
    hf_late = hfp + _derived_zero(anchor).astype(jnp.bfloat16)
    def sort_rows(p, late_hf, r0, n_rows):
        rows = jnp.dot(p, late_hf, preferred_element_type=jnp.float32)
        stage_ref[st_sort, r0 // CHUNK:(r0 + n_rows) // CHUNK] = (
            rows.astype(jnp.bfloat16).reshape(n_rows // CHUNK, CHUNK, x_ref.shape[-1]))

    for r0 in range(0, COMMON_ROWS, PERM_ROWS):
        sort_rows(perm_ref[r0:r0 + PERM_ROWS, :], hf_late, r0, PERM_ROWS)

    conv = convw_ref[CONV_WIDTH - 1:CONV_WIDTH, :] * extu_ref[HALO:, :]
    for k in range(CONV_WIDTH - 1):
        off = HALO - (CONV_WIDTH - 1 - k)
        conv = conv + convw_ref[k:k + 1, :] * extu_ref[off:off + tm, :]
    y_conv = _rms(b_gate * conv, nconv_ref[...])

    mixed = []
    for g, w in enumerate(POOL_WINDOWS):
        cols = slice(g * LANES, (g + 1) * LANES)
        acc = extp_ref[:, cols]
        span = 1
        while span < w:
            acc = acc + pltpu.roll(acc, span, 0)
            span *= 2
        pooled = acc[HALO:] * (1.0 / w) - v_pool[:, cols]
        mixed.append(jnp.dot(pooled.astype(jnp.bfloat16), poolw_ref[g],
                             preferred_element_type=jnp.float32))
    y_pool = _rms(jnp.concatenate(mixed, axis=-1) * pscale_ref[...], npool_ref[...])

    y = jnp.concatenate([y_conv, y_pool], axis=-1).astype(jnp.bfloat16)
    h1 = x + jnp.dot(y, wout_ref[...], preferred_element_type=jnp.float32)
    h1_ref[...] = h1
    h1s_ref[...] = h1
    hn_ref[1] = hn_next

    extu_ref[0:HALO, :] = extu_ref[tm:tm + HALO, :]
    extp_ref[0:HALO, :] = extp_ref[tm:tm + HALO, :]

    pending = pend_ref[0]
    tile2 = jnp.maximum(s - 2, 0)
    tot = 0
    issued = 0
    most = 0
    plan = []
    for e in range(N_EXPERTS):
        nch = info_s_ref[cur, 0, e]
        base = run_ref[e]
        tab_ref[tile2 * TAB_STRIDE + e] = base
        tab_ref[tile2 * TAB_STRIDE + N_EXPERTS + e] = nch
        run_ref[e] = base + nch
        plan.append((nch > 0, info_s_ref[cur, 1, e], e * cap_chunks + base))
        tot = tot + nch
        issued = issued + jnp.where(nch > 0, SEGMENT_CHUNKS, 0) + jnp.maximum(nch - SEGMENT_CHUNKS, 0)
        most = jnp.maximum(most, nch)
    tab_ref[tile2 * TAB_STRIDE + 2 * N_EXPERTS] = tot
    pend_ref[0] = issued
    _drain(pending, zero_ref, xs_ref, sem_out)
    for nonempty, src0, dst0 in plan:
        @pl.when(nonempty)
        def _():
            _copy_chunks(stage_ref.at[st_disp], src0, xs_ref, dst0, sem_out, SEGMENT_CHUNKS).start()

    @pl.when(n_sorted > COMMON_ROWS)
    def _():
        sort_rows(sort_matrix(COMMON_ROWS, STAGE_ROWS - COMMON_ROWS), hfp,
                  COMMON_ROWS, STAGE_ROWS - COMMON_ROWS)

    @pl.when(most > SEGMENT_CHUNKS)
    def _():
        def body(e, c):
            _issue_rest(SEGMENT_CHUNKS, info_s_ref[cur, 0, e],
                        stage_ref.at[st_disp], info_s_ref[cur, 1, e],
                        xs_ref, e * cap_chunks + tab_ref[tile2 * TAB_STRIDE + e], sem_out)
            return c
        lax.fori_loop(0, N_EXPERTS, body, 0)

    @pl.when(s == n_tiles)
    def _():
        info_copy(prv).wait()
        _drain(pend_ref[0], zero_ref, xs_ref, sem_out)
        sem_last = sem_out

        def last_body(e, tot):
            nch = info_s_ref[prv, 0, e]
            base = run_ref[e]
            tab_ref[(s - 1) * TAB_STRIDE + e] = base
            tab_ref[(s - 1) * TAB_STRIDE + N_EXPERTS + e] = nch
            _issue_rest(0, nch, stage_ref.at[st_sort], info_s_ref[prv, 1, e],
                        xs_ref, e * cap_chunks + base, sem_last)
            run_ref[e] = base + nch
            return tot + nch
        last = lax.fori_loop(0, N_EXPERTS, last_body, 0)
        tab_ref[(s - 1) * TAB_STRIDE + 2 * N_EXPERTS] = last
        _drain(last, zero_ref, xs_ref, sem_last)

        subs_per_tile = MOE_TILE // MOE_SUB

        def pad_body(e, carry):
            tot, w = carry
            n = run_ref[e]
            subs = (n + (MOE_SUB_CHUNKS - 1)) // MOE_SUB_CHUNKS
            tiles = (subs + (subs_per_tile - 1)) // subs_per_tile
            nch = subs * MOE_SUB_CHUNKS - n

            def chunk_body(k, c):
                _copy_chunks(zero_ref, 0, xs_ref, e * cap_chunks + n + k, sem_last).start()
                return c
            lax.fori_loop(0, nch, chunk_body, 0)

            def work_body(j, c):
                wexp_ref[w + j] = e
                wblk_ref[w + j] = e * (cap_chunks // MOE_CHUNKS) + j
                wsub_ref[w + j] = jnp.minimum(subs - j * subs_per_tile, subs_per_tile)
                return c
            lax.fori_loop(0, tiles, work_body, 0)
            return tot + nch, w + tiles
        n_pad, n_work = lax.fori_loop(0, N_EXPERTS, pad_body, (0, 0))
        _drain(n_pad, zero_ref, xs_ref, sem_last)
        nwork_ref[0] = n_work

        def next_body(k, nxt):
            w = n_work - 1 - k
            wnxt_ref[w] = nxt
            prev = wexp_ref[jnp.maximum(w - 1, 0)]
            return jnp.where(prev != wexp_ref[w], wexp_ref[w], nxt)
        lax.fori_loop(0, n_work, next_body, wexp_ref[n_work - 1])

        def fill_body(w, c):
            wexp_ref[w] = 0
            wblk_ref[w] = 0
            wnxt_ref[w] = 0
            wsub_ref[w] = 0
            return c
        lax.fori_loop(n_work, max_work, fill_body, 0)


def _moe_kernel(exp_ref, blk_ref, nxt_ref, sub_ref, n_ref, xs_ref, wg_ref, wu_ref, wd_ref, ys_ref,
                xbuf_ref, ybuf_ref, wgf_ref, wuf_ref, wdf_ref, wgu_bf_ref, wd_bf_ref,
                sem_x, sem_y, sem_w, *, d_exp, tile):
    n = n_ref[0]

    def x_copy(w, slot):
        return pltpu.make_async_copy(xs_ref.at[pl.ds(blk_ref[w] * tile, tile)],
                                     xbuf_ref.at[slot], sem_x.at[slot])

    def y_copy(w, slot):
        return pltpu.make_async_copy(ybuf_ref.at[slot],
                                     ys_ref.at[pl.ds(blk_ref[w] * tile, tile)], sem_y.at[slot])

    def w_copies(e, slot):
        return (pltpu.make_async_copy(wg_ref.at[e], wgf_ref.at[slot], sem_w.at[slot]),
                pltpu.make_async_copy(wu_ref.at[e], wuf_ref.at[slot], sem_w.at[slot]),
                pltpu.make_async_copy(wd_ref.at[e], wdf_ref.at[slot], sem_w.at[slot]))

    for c in w_copies(exp_ref[0], 0):
        c.start()
    x_copy(0, 0).start()

    @pl.when(n > 1)
    def _():
        x_copy(1, 1).start()

    def item(w, run):
        e = exp_ref[w]
        first = (w == 0) | (exp_ref[jnp.maximum(w - 1, 0)] != e)
        wslot = run % 2
        xslot = w % 3
        yslot = w % 2

        @pl.when(first)
        def _():
            for c in w_copies(e, wslot):
                c.wait()
            wgu_bf_ref[:, :d_exp] = wgf_ref[wslot].astype(jnp.bfloat16)
            wgu_bf_ref[:, d_exp:] = wuf_ref[wslot].astype(jnp.bfloat16)
            wd_bf_ref[...] = wdf_ref[wslot].astype(jnp.bfloat16)

            @pl.when(nxt_ref[w] != e)
            def _():
                for c in w_copies(nxt_ref[w], 1 - wslot):
                    c.start()

        x_copy(w, xslot).wait()

        @pl.when(w + 2 < n)
        def _():
            x_copy(w + 2, (w + 2) % 3).start()

        @pl.when(w >= 2)
        def _():
            y_copy(w - 2, yslot).wait()

        for k in range(1, tile // MOE_SUB + 1):
            @pl.when(sub_ref[w] == k)
            def _():
                rows = k * MOE_SUB
                gu = jnp.dot(xbuf_ref[xslot, 0:rows], wgu_bf_ref[...],
                             preferred_element_type=jnp.float32)
                g = gu[:, :d_exp]
                a = g * jax.nn.sigmoid(g) * gu[:, d_exp:]
                ybuf_ref[yslot, 0:rows] = jnp.dot(
                    a.astype(jnp.bfloat16), wd_bf_ref[...],
                    preferred_element_type=jnp.float32).astype(jnp.bfloat16)
        y_copy(w, yslot).start()
        return run + first.astype(jnp.int32)
    lax.fori_loop(0, n, item, 0)

    @pl.when(n >= 2)
    def _():
        y_copy(n - 2, n % 2).wait()
    y_copy(n - 1, (n - 1) % 2).wait()


def _combine_kernel(tab_ref, h1_ref, route_ref, fn_ref, ys_ref, out_ref,
                    h1buf_ref, rbuf_ref, obuf_ref, stage_ref, sem_in, sem_out, sem,
                    *, tm, cap_chunks, n_tiles):
    d = obuf_ref.shape[-1]

    def in_copies(j, b):
        rows = pl.ds(j * tm, tm)
        return (pltpu.make_async_copy(h1_ref.at[rows], h1buf_ref.at[b], sem_in.at[b]),
                pltpu.make_async_copy(route_ref.at[rows], rbuf_ref.at[b], sem_in.at[b]))

    def out_copy(j, b):
        return pltpu.make_async_copy(obuf_ref.at[b], out_ref.at[pl.ds(j * tm, tm)], sem_out.at[b])

    stage_ref[...] = jnp.zeros_like(stage_ref)
    for j in range(2):
        def body(e, coff):
            nch = tab_ref[j * TAB_STRIDE + N_EXPERTS + e]
            _issue_rest(0, nch, ys_ref, e * cap_chunks + tab_ref[j * TAB_STRIDE + e],
                        stage_ref.at[j], coff, sem.at[j])
            return coff + nch
        lax.fori_loop(0, N_EXPERTS, body, 0)
        for c in in_copies(j, j):
            c.start()

    def tile_body(s, carry):
        slot = s % 3
        ahead = (s + 2) % 3
        b = s % 2
        n_chunks = tab_ref[s * TAB_STRIDE + 2 * N_EXPERTS]
        _drain(n_chunks, ys_ref, stage_ref.at[slot], sem.at[slot])
        for c in in_copies(s, b):
            c.wait()

        @pl.when(s >= 2)
        def _():
            out_copy(s - 2, b).wait()

        route = rbuf_ref[b]
        pos0 = route[:, 0:1].astype(jnp.int32)
        pos1 = route[:, 1:2].astype(jnp.int32)
        gate0 = route[:, 2:3]
        gate1 = route[:, 3:4]

        def combine(n_rows):
            r = lax.broadcasted_iota(jnp.int32, (tm, n_rows), 1)
            pw = jnp.where(r == pos0, gate0, jnp.where(r == pos1, gate1, 0.0)).astype(jnp.bfloat16)
            rows = stage_ref[slot, 0:n_rows // CHUNK].reshape(n_rows, d)
            moe = jnp.dot(pw, rows, preferred_element_type=jnp.float32)
            obuf_ref[b] = _rms(h1buf_ref[b] + moe, fn_ref[...])

        @pl.when(n_chunks <= COMMON_ROWS // CHUNK)
        def _():
            combine(COMMON_ROWS)

        @pl.when(n_chunks > COMMON_ROWS // CHUNK)
        def _():
            combine(STAGE_ROWS)

        out_copy(s, b).start()

        @pl.when(s + 2 < n_tiles)
        def _():
            for c in in_copies(s + 2, b):
                c.start()
        nxt = jnp.minimum(s + 2, n_tiles - 1)
        has_next = jnp.where(s + 2 < n_tiles, 1, 0)
        coff = 0
        most = 0
        plan = []
        for e in range(N_EXPERTS):
            nch = tab_ref[nxt * TAB_STRIDE + N_EXPERTS + e] * has_next
            plan.append((jnp.minimum(nch, UNROLLED_CHUNKS),
                         e * cap_chunks + tab_ref[nxt * TAB_STRIDE + e], coff))
            coff = coff + nch
            most = jnp.maximum(most, nch)
        for head, src0, dst0 in plan:
            _issue_head(head, ys_ref, src0, stage_ref.at[ahead], dst0, sem.at[ahead])

        @pl.when(most > UNROLLED_CHUNKS)
        def _():
            def body(e, coff):
                nch = tab_ref[nxt * TAB_STRIDE + N_EXPERTS + e]
                _issue_rest(UNROLLED_CHUNKS, nch, ys_ref,
                            e * cap_chunks + tab_ref[nxt * TAB_STRIDE + e],
                            stage_ref.at[ahead], coff, sem.at[ahead])
                return coff + nch
            lax.fori_loop(0, N_EXPERTS, body, 0)
        return carry
    lax.fori_loop(0, n_tiles, tile_body, 0)

    out_copy(n_tiles - 2, n_tiles % 2).wait()
    out_copy(n_tiles - 1, (n_tiles - 1) % 2).wait()


def kernel(x, meta_tokens, norm_mix, w_in, conv_w, norm_conv_out, pool_w, pool_scale,
           norm_pool_out, w_out, norm_ffn, w_router_group, w_router_expert,
           w_gate, w_up, w_down, final_norm):
    bsz, seq, d = x.shape
    d_conv = conv_w.shape[-1]
    d_exp = w_gate.shape[-1]
    n_exp = w_gate.shape[1]
    assert meta_tokens.shape[0] == N_META == HALO and n_exp == N_EXPERTS
    assert seq % MIX_TILE == 0
    bf = jnp.bfloat16
    i32 = jnp.int32
    t = bsz * seq
    n_tiles = t // MIX_TILE
    cap = -(-(t + CHUNK * n_tiles) // MOE_TILE) * MOE_TILE + MOE_TILE
    n_chunks = N_EXPERTS * cap // CHUNK
    max_work = (2 * t + N_EXPERTS * (CHUNK - 1) * n_tiles) // MOE_TILE + N_EXPERTS
    row = lambda a: a.reshape(1, -1)
    wr = jnp.concatenate(
        [w_router_expert[0], w_router_group[0],
         jnp.zeros((d, LANES - N_EXPERTS - N_EXPERT_GROUPS), jnp.float32)], axis=1).astype(bf)

    full = lambda shape: pl.BlockSpec(shape, lambda s: (0,) * len(shape))
    last = n_tiles - 1
    smem = pl.BlockSpec(memory_space=pltpu.SMEM)
    x2 = x.reshape(t, d)
    h1, route, xs, tab, wexp, wblk, wnxt, wsub, nwork = pl.pallas_call(
        functools.partial(_mixer_kernel, tm=MIX_TILE, d_conv=d_conv, cap_chunks=cap // CHUNK,
                          tiles_per_seq=seq // MIX_TILE, max_work=max_work),
        grid=(n_tiles + 1,),
        in_specs=[pl.BlockSpec((MIX_TILE, d), lambda s: (jnp.minimum(s, last), 0)),
                  pl.BlockSpec((MIX_TILE, d), lambda s: (jnp.minimum(s + 1, last), 0)),
                  full((N_META, d)), full((1, d)), full((d, 4 * d_conv)),
                  full((CONV_WIDTH, d_conv)), full((1, d_conv)),
                  full((len(POOL_WINDOWS), LANES, LANES)), full((1, d_conv)), full((1, d_conv)),
                  full((2 * d_conv, d)), full((1, d)), full((d, LANES))],
        out_specs=[pl.BlockSpec((MIX_TILE, d), lambda s: (s, 0)),
                   pl.BlockSpec((MIX_TILE, LANES), lambda s: (jnp.maximum(s - 1, 0), 0)),
                   pl.BlockSpec(memory_space=pl.ANY), smem, smem, smem, smem, smem, smem],
        out_shape=[jax.ShapeDtypeStruct((t + MIX_TILE, d), jnp.float32),
                   jax.ShapeDtypeStruct((t, LANES), jnp.float32),
                   jax.ShapeDtypeStruct((n_chunks, CHUNK, d), bf),
                   jax.ShapeDtypeStruct((n_tiles * TAB_STRIDE,), i32),
                   jax.ShapeDtypeStruct((max_work,), i32),
                   jax.ShapeDtypeStruct((max_work,), i32),
                   jax.ShapeDtypeStruct((max_work,), i32),
                   jax.ShapeDtypeStruct((max_work,), i32),
                   jax.ShapeDtypeStruct((1,), i32)],
        scratch_shapes=[pltpu.VMEM((MIX_TILE + HALO, d_conv), jnp.float32),
                        pltpu.VMEM((MIX_TILE + HALO, d_conv), jnp.float32),
                        pltpu.VMEM((2, MIX_TILE, d), bf),
                        pltpu.VMEM((MIX_TILE, d), jnp.float32),
                        pltpu.VMEM((STAGE_ROWS, MIX_TILE), bf),
                        pltpu.VMEM((3, STAGE_CHUNKS + SEGMENT_CHUNKS, CHUNK, d), bf),
                        pltpu.VMEM((WAIT_GROUP, CHUNK, d), bf),
                        pltpu.VMEM((SUBLANES, LANES), i32),
                        pltpu.SMEM((2, SUBLANES, LANES), i32),
                        pltpu.SMEM((N_EXPERTS,), i32),
                        pltpu.SMEM((1,), i32),
                        pltpu.SemaphoreType.DMA,
                        pltpu.SemaphoreType.DMA],
        compiler_params=pltpu.CompilerParams(
            dimension_semantics=("arbitrary",), vmem_limit_bytes=VMEM_LIMIT),
        name="mixer",
    )(x2, x2, meta_tokens, row(norm_mix), w_in[0].astype(bf), conv_w[0],
      row(norm_conv_out), pool_w[0].astype(bf), row(pool_scale), row(norm_pool_out),
      w_out[0].astype(bf), row(norm_ffn), wr)

    anyspec = pl.BlockSpec(memory_space=pl.ANY)
    f32 = jnp.float32
    ys = pl.pallas_call(
        functools.partial(_moe_kernel, d_exp=d_exp, tile=MOE_TILE),
        grid_spec=pltpu.PrefetchScalarGridSpec(
            num_scalar_prefetch=5,
            grid=(1,),
            in_specs=[anyspec, anyspec, anyspec, anyspec],
            out_specs=anyspec,
            scratch_shapes=[pltpu.VMEM((3, MOE_TILE, d), bf), pltpu.VMEM((2, MOE_TILE, d), bf),
                            pltpu.VMEM((2, d, d_exp), f32), pltpu.VMEM((2, d, d_exp), f32),
                            pltpu.VMEM((2, d_exp, d), f32),
                            pltpu.VMEM((d, 2 * d_exp), bf), pltpu.VMEM((d_exp, d), bf),
                            pltpu.SemaphoreType.DMA((3,)), pltpu.SemaphoreType.DMA((2,)),
                            pltpu.SemaphoreType.DMA((2,))]),
        out_shape=jax.ShapeDtypeStruct((n_chunks * CHUNK, d), bf),
        compiler_params=pltpu.CompilerParams(
            dimension_semantics=("arbitrary",), vmem_limit_bytes=VMEM_LIMIT),
        name="moe",
    )(wexp, wblk, wnxt, wsub, nwork, xs.reshape(n_chunks * CHUNK, d), w_gate[0], w_up[0], w_down[0])

    out = pl.pallas_call(
        functools.partial(_combine_kernel, tm=MIX_TILE, cap_chunks=cap // CHUNK, n_tiles=n_tiles),
        grid_spec=pltpu.PrefetchScalarGridSpec(
            num_scalar_prefetch=1,
            grid=(1,),
            in_specs=[anyspec, anyspec, pl.BlockSpec((1, d), lambda s, tb: (0, 0)), anyspec],
            out_specs=anyspec,
            scratch_shapes=[pltpu.VMEM((2, MIX_TILE, d), f32), pltpu.VMEM((2, MIX_TILE, LANES), f32),
                            pltpu.VMEM((2, MIX_TILE, d), f32),
                            pltpu.VMEM((3, STAGE_CHUNKS, CHUNK, d), bf),
                            pltpu.SemaphoreType.DMA((2,)), pltpu.SemaphoreType.DMA((2,)),
                            pltpu.SemaphoreType.DMA((3,))]),
        out_shape=jax.ShapeDtypeStruct((t, d), jnp.float32),
        compiler_params=pltpu.CompilerParams(
            dimension_semantics=("arbitrary",), vmem_limit_bytes=VMEM_LIMIT),
        name="combine",
    )(tab, h1, route, row(final_norm), ys.reshape(n_chunks, CHUNK, d))
    return out.reshape(bsz, seq, d)
```
